```python
import math
import jax, jax.numpy as jnp
from jax import lax
import numpy as np

D_MODEL = 2048
BATCH = 2
SEQ = 16384
DEPTH = 2

D_MIX = D_MODEL
POOL_WIDTH = D_MIX // 4
POOL_WINDOWS = (2, 4, 8, 16)
POOL_GROUPS = len(POOL_WINDOWS)
POOL_GROUP_DIM = POOL_WIDTH // POOL_GROUPS
SSM_WIDTH = D_MIX // 4
SSM_GROUP = 16
SSM_N_GROUPS = SSM_WIDTH // SSM_GROUP
SSM_STATE = 64
DT_MIN = 0.001
DT_MAX = 0.1
MLA_HEADS = 8
QK_NOPE = 128
QK_ROPE = 64
QK_HEAD = QK_NOPE + QK_ROPE
V_HEAD = 128
MLA_WIDTH = MLA_HEADS * V_HEAD
Q_LORA = 512
KV_LORA = 256
ROPE_THETA = 10000.0
Q_BLOCK = 128
IN_COLS = POOL_WIDTH + SSM_WIDTH + Q_LORA + KV_LORA + QK_ROPE
MOE_GROUPS = 4
EXPERTS_PER_GROUP = 8
N_EXPERTS = MOE_GROUPS * EXPERTS_PER_GROUP
TOP_K = 2
D_EXPERT = 512
MOE_BLOCK = 128
EPS = 1e-6

kernel_name = 'hybrid_pool_s5_mla_hmoe'


def rms_norm(x, g):
    xf = x.astype(jnp.float32)
    y = xf * lax.rsqrt(jnp.mean(xf * xf, axis=-1, keepdims=True) + EPS)
    return (y * g.astype(jnp.float32)).astype(x.dtype)


def apply_rope(x, cos, sin):
    xf = x.astype(jnp.float32)
    x1, x2 = jnp.split(xf, 2, axis=-1)
    return jnp.concatenate([x1 * cos - x2 * sin, x2 * cos + x1 * sin], axis=-1).astype(x.dtype)


def pool_mixer(u, w_lin, scale):
    B_, L, _ = u.shape
    uf = u.astype(jnp.float32).reshape(B_, L, POOL_GROUPS, POOL_GROUP_DIM)
    cs = jnp.cumsum(uf, axis=1)
    t = jnp.arange(L)
    outs = []
    for gi, w in enumerate(POOL_WINDOWS):
        c_g = cs[:, :, gi]
        lag = jnp.pad(c_g, ((0, 0), (w, 0), (0, 0)))[:, :L]
        cnt = jnp.minimum(t + 1, w).astype(jnp.float32)[None, :, None]
        outs.append((c_g - lag) / cnt - uf[:, :, gi])
    d = jnp.stack(outs, axis=2).astype(u.dtype)
    y = jnp.einsum('blgc,gcd->blgd', d, w_lin)
    return y.reshape(B_, L, POOL_WIDTH) * scale


def _complex_scan_op(e1, e2):
    a1r, a1i, b1r, b1i = e1
    a2r, a2i, b2r, b2i = e2
    return (a2r * a1r - a2i * a1i,
            a2r * a1i + a2i * a1r,
            a2r * b1r - a2i * b1i + b2r,
            a2r * b1i + a2i * b1r + b2i)


def s5_mixer(u, lam_re, lam_im, log_dt, b_re, b_im, c_re, c_im, d_skip, w_glu):
    B_, L, _ = u.shape
    f32 = jnp.float32
    uf = u.astype(f32).reshape(B_, L, SSM_N_GROUPS, SSM_GROUP)
    dt = jnp.exp(log_dt.astype(f32))[:, None]
    lr, li = lam_re.astype(f32), lam_im.astype(f32)
    mag = jnp.exp(lr * dt)
    ab_re, ab_im = mag * jnp.cos(li * dt), mag * jnp.sin(li * dt)
    den = lr * lr + li * li
    f_re = ((ab_re - 1.0) * lr + ab_im * li) / den
    f_im = (ab_im * lr - (ab_re - 1.0) * li) / den
    br, bi = b_re.astype(f32), b_im.astype(f32)
    bb_re = f_re[..., None] * br - f_im[..., None] * bi
    bb_im = f_re[..., None] * bi + f_im[..., None] * br
    bu_re = jnp.einsum('blgh,gph->blgp', uf, bb_re)
    bu_im = jnp.einsum('blgh,gph->blgp', uf, bb_im)
    a_re = jnp.broadcast_to(ab_re, bu_re.shape)
    a_im = jnp.broadcast_to(ab_im, bu_im.shape)
    _, _, s_re, s_im = lax.associative_scan(_complex_scan_op, (a_re, a_im, bu_re, bu_im), axis=1)
    y = (jnp.einsum('blgp,ghp->blgh', s_re, c_re.astype(f32))
         - jnp.einsum('blgp,ghp->blgh', s_im, c_im.astype(f32)))
    y = y.reshape(B_, L, SSM_WIDTH) + d_skip.astype(f32) * uf.reshape(B_, L, SSM_WIDTH)
    y = jax.nn.gelu(y).astype(u.dtype)
    return y * jax.nn.sigmoid(y @ w_glu)


def mla_mixer(q_c, kv_c, k_pe_in, q_norm_g, kv_norm_g, w_uq, w_ukv, cos, sin):
    B_, L, _ = q_c.shape
    q = jnp.einsum('blr,rhd->blhd', rms_norm(q_c, q_norm_g), w_uq)
    q_nope = q[..., :QK_NOPE]
    q_pe = apply_rope(q[..., QK_NOPE:], cos[:, :, None, :], sin[:, :, None, :])
    kv = jnp.einsum('blr,rhd->blhd', rms_norm(kv_c, kv_norm_g), w_ukv)
    k_nope, v = kv[..., :QK_NOPE], kv[..., QK_NOPE:]
    k_pe = apply_rope(k_pe_in, cos, sin)
    scale = QK_HEAD ** -0.5
    n_blocks = L // Q_BLOCK
    key_idx = jnp.arange(L)

    def attend_block(i):
        s = i * Q_BLOCK
        qn = lax.dynamic_slice_in_dim(q_nope, s, Q_BLOCK, axis=1)
        qp = lax.dynamic_slice_in_dim(q_pe, s, Q_BLOCK, axis=1)
        sc = (jnp.einsum('bqhd,bkhd->bhqk', qn, k_nope)
              + jnp.einsum('bqhd,bkd->bhqk', qp, k_pe)).astype(jnp.float32) * scale
        causal = key_idx[None, :] <= (s + jnp.arange(Q_BLOCK))[:, None]
        sc = jnp.where(causal[None, None], sc, -jnp.inf)
        p = jax.nn.softmax(sc, axis=-1).astype(v.dtype)
        return jnp.einsum('bhqk,bkhd->bqhd', p, v)

    o = lax.map(attend_block, jnp.arange(n_blocks))
    return o.transpose(1, 0, 2, 3, 4).reshape(B_, L, MLA_WIDTH)


def hier_moe(h, w_rg, b_rg, w_re, b_re, w1, w3, w2):
    B_, L, D = h.shape
    N = B_ * L
    hf = h.reshape(N, D)
    g_logits = (hf @ w_rg).astype(jnp.float32) + b_rg
    g_prob = jax.nn.softmax(g_logits, axis=-1)
    g_top = jnp.argmax(g_logits, axis=-1)
    g_w = jnp.take_along_axis(g_prob, g_top[:, None], axis=-1)
    e_logits = ((hf @ w_re).astype(jnp.float32) + b_re).reshape(N, MOE_GROUPS, EXPERTS_PER_GROUP)
    e_logits = jnp.take_along_axis(e_logits, g_top[:, None, None], axis=1)[:, 0]
    e_prob = jax.nn.softmax(e_logits, axis=-1)
    top_p, top_i = lax.top_k(e_prob, TOP_K)
    top_p = top_p / jnp.sum(top_p, axis=-1, keepdims=True)
    weights = (g_w * top_p).reshape(-1)
    expert = (g_top[:, None] * EXPERTS_PER_GROUP + top_i).reshape(-1).astype(jnp.int32)
    token = jnp.repeat(jnp.arange(N, dtype=jnp.int32), TOP_K)
    A = N * TOP_K
    order = jnp.argsort(expert)
    e_s, tok_s, w_s = expert[order], token[order], weights[order]
    counts = jnp.bincount(expert, length=N_EXPERTS)
    padded = (counts + MOE_BLOCK - 1) // MOE_BLOCK * MOE_BLOCK
    pad_end = jnp.cumsum(padded)
    pad_start = pad_end - padded
    start = jnp.cumsum(counts) - counts
    dest = pad_start[e_s] + (jnp.arange(A, dtype=jnp.int32) - start[e_s])
    P = (-(-A // MOE_BLOCK) + N_EXPERTS) * MOE_BLOCK
    n_blocks = P // MOE_BLOCK
    row_tok = jnp.zeros((P,), jnp.int32).at[dest].set(tok_s)
    row_w = jnp.zeros((P,), jnp.float32).at[dest].set(w_s)
    block_start = jnp.arange(n_blocks, dtype=jnp.int32) * MOE_BLOCK
    block_expert = jnp.minimum(jnp.searchsorted(pad_end, block_start, side='right'), N_EXPERTS - 1)

    def expert_block(args):
        tok, e = args
        xb = hf[tok]
        a = jax.nn.silu(xb @ w1[e]) * (xb @ w3[e])
        return a @ w2[e]

    out = lax.map(expert_block, (row_tok.reshape(n_blocks, MOE_BLOCK), block_expert))
    out = out.reshape(P, D) * row_w[:, None].astype(out.dtype)
    y = jax.ops.segment_sum(out, row_tok, num_segments=N)
    return y.reshape(B_, L, D).astype(h.dtype)


def setup_inputs(seed: int = 0) -> dict:
    key = jax.random.key(seed)
    ks = iter(jax.random.split(key, 64))
    f32 = jnp.float32

    def nrm(shape, scale):
        return jax.random.normal(next(ks), shape, f32) * scale

    def gain(shape):
        return 1.0 + nrm(shape, 0.02)

    Lr = DEPTH
    x = nrm((BATCH, SEQ, D_MODEL), 1.0)
    c = nrm((BATCH, D_MODEL), 1.0)
    offset = jax.random.randint(next(ks), (BATCH, 1), 0, 4096, jnp.int32)
    positions = (jnp.arange(SEQ, dtype=jnp.int32)[None, :] + offset).astype(jnp.int32)
    w_ada = nrm((Lr, D_MODEL, 6 * D_MODEL), 0.3 * D_MODEL ** -0.5)
    b_ada = nrm((Lr, 6 * D_MODEL), 0.02)
    norm1_g = gain((Lr, D_MODEL))
    w_in = nrm((Lr, D_MODEL, IN_COLS), D_MODEL ** -0.5)
    pool_w = nrm((Lr, POOL_GROUPS, POOL_GROUP_DIM, POOL_GROUP_DIM), POOL_GROUP_DIM ** -0.5)
    pool_scale = 1.0 + nrm((Lr, POOL_WIDTH), 0.1)
    n = jnp.arange(SSM_STATE, dtype=f32)
    ssm_lam_re = -0.5 + nrm((Lr, SSM_N_GROUPS, SSM_STATE), 1e-3)
    ssm_lam_im = math.pi * n + nrm((Lr, SSM_N_GROUPS, SSM_STATE), 1e-3)
    ssm_log_dt = jax.random.uniform(next(ks), (Lr, SSM_N_GROUPS), f32, math.log(DT_MIN), math.log(DT_MAX))
    ssm_b_re = nrm((Lr, SSM_N_GROUPS, SSM_STATE, SSM_GROUP), (2 * SSM_GROUP) ** -0.5)
    ssm_b_im = nrm((Lr, SSM_N_GROUPS, SSM_STATE, SSM_GROUP), (2 * SSM_GROUP) ** -0.5)
    ssm_c_re = nrm((Lr, SSM_N_GROUPS, SSM_GROUP, SSM_STATE), (2 * SSM_STATE) ** -0.5)
    ssm_c_im = nrm((Lr, SSM_N_GROUPS, SSM_GROUP, SSM_STATE), (2 * SSM_STATE) ** -0.5)
    ssm_d = nrm((Lr, SSM_WIDTH), 1.0)
    ssm_w_glu = nrm((Lr, SSM_WIDTH, SSM_WIDTH), SSM_WIDTH ** -0.5)
    q_norm_g = gain((Lr, Q_LORA))
    kv_norm_g = gain((Lr, KV_LORA))
    w_uq = nrm((Lr, Q_LORA, MLA_HEADS, QK_HEAD), Q_LORA ** -0.5)
    w_ukv = nrm((Lr, KV_LORA, MLA_HEADS, QK_NOPE + V_HEAD), KV_LORA ** -0.5)
    out_norm_g = gain((Lr, D_MIX))
    w_out = nrm((Lr, D_MIX, D_MODEL), D_MIX ** -0.5)
    norm2_g = gain((Lr, D_MODEL))
    router_w_group = nrm((Lr, D_MODEL, MOE_GROUPS), D_MODEL ** -0.5)
    router_b_group = nrm((Lr, MOE_GROUPS), 0.01)
    router_w_expert = nrm((Lr, D_MODEL, N_EXPERTS), D_MODEL ** -0.5)
    router_b_expert = nrm((Lr, N_EXPERTS), 0.01)
    w_gate = nrm((Lr, N_EXPERTS, D_MODEL, D_EXPERT), D_MODEL ** -0.5)
    w_up = nrm((Lr, N_EXPERTS, D_MODEL, D_EXPERT), D_MODEL ** -0.5)
    w_down = nrm((Lr, N_EXPERTS, D_EXPERT, D_MODEL), D_EXPERT ** -0.5)
    final_g = gain((D_MODEL,))
    return {'x': x, 'c': c, 'positions': positions, 'w_ada': w_ada, 'b_ada': b_ada,
            'norm1_g': norm1_g, 'w_in': w_in, 'pool_w': pool_w, 'pool_scale': pool_scale,
            'ssm_lam_re': ssm_lam_re, 'ssm_lam_im': ssm_lam_im, 'ssm_log_dt': ssm_log_dt,
            'ssm_b_re': ssm_b_re, 'ssm_b_im': ssm_b_im, 'ssm_c_re': ssm_c_re, 'ssm_c_im': ssm_c_im,
            'ssm_d': ssm_d, 'ssm_w_glu': ssm_w_glu, 'q_norm_g': q_norm_g, 'kv_norm_g': kv_norm_g,
            'w_uq': w_uq, 'w_ukv': w_ukv, 'out_norm_g': out_norm_g, 'w_out': w_out,
            'norm2_g': norm2_g, 'router_w_group': router_w_group, 'router_b_group': router_b_group,
            'router_w_expert': router_w_expert, 'router_b_expert': router_b_expert,
            'w_gate': w_gate, 'w_up': w_up, 'w_down': w_down, 'final_g': final_g}


def reference(x, c, positions, w_ada, b_ada, norm1_g, w_in, pool_w, pool_scale,
              ssm_lam_re, ssm_lam_im, ssm_log_dt, ssm_b_re, ssm_b_im, ssm_c_re, ssm_c_im,
              ssm_d, ssm_w_glu, q_norm_g, kv_norm_g, w_uq, w_ukv, out_norm_g, w_out,
              norm2_g, router_w_group, router_b_group, router_w_expert, router_b_expert,
              w_gate, w_up, w_down, final_g):
    inv_freq = jnp.power(ROPE_THETA, -jnp.arange(0, QK_ROPE, 2, dtype=jnp.float32) / QK_ROPE)
    ang = positions.astype(jnp.float32)[..., None] * inv_freq
    cos, sin = jnp.cos(ang), jnp.sin(ang)
    in_splits = (POOL_WIDTH, POOL_WIDTH + SSM_WIDTH, POOL_WIDTH + SSM_WIDTH + Q_LORA,
                 POOL_WIDTH + SSM_WIDTH + Q_LORA + KV_LORA)
    for l in range(DEPTH):
        mod = c @ w_ada[l] + b_ada[l]
        sh_a, sc_a, g_a, sh_f, sc_f, g_f = [m[:, None, :] for m in jnp.split(mod, 6, axis=-1)]
        h = rms_norm(x, norm1_g[l]) * (1.0 + sc_a) + sh_a
        z = h @ w_in[l]
        u_pool, u_ssm, q_c, kv_c, k_pe = jnp.split(z, in_splits, axis=-1)
        y_pool = pool_mixer(u_pool, pool_w[l], pool_scale[l])
        y_ssm = s5_mixer(u_ssm, ssm_lam_re[l], ssm_lam_im[l], ssm_log_dt[l], ssm_b_re[l], ssm_b_im[l],
                         ssm_c_re[l], ssm_c_im[l], ssm_d[l], ssm_w_glu[l])
        y_mla = mla_mixer(q_c, kv_c, k_pe, q_norm_g[l], kv_norm_g[l], w_uq[l], w_ukv[l], cos, sin)
        gn = out_norm_g[l]
        y = jnp.concatenate([rms_norm(y_pool, gn[:POOL_WIDTH]),
                             rms_norm(y_ssm, gn[POOL_WIDTH:POOL_WIDTH + SSM_WIDTH]),
                             rms_norm(y_mla, gn[POOL_WIDTH + SSM_WIDTH:])], axis=-1)
        x = x + g_a * (y @ w_out[l])
        h = rms_norm(x, norm2_g[l]) * (1.0 + sc_f) + sh_f
        x = x + g_f * hier_moe(h, router_w_group[l], router_b_group[l], router_w_expert[l],
                               router_b_expert[l], w_gate[l], w_up[l], w_down[l])
    return rms_norm(x, final_g)
```

```python
import functools
import math

import jax
import jax.numpy as jnp
from jax import lax
from jax.experimental import pallas as pl
from jax.experimental.pallas import tpu as pltpu

F32 = jnp.float32
BF16 = jnp.bfloat16

POOL_WINDOWS = (2, 4, 8, 16)
SSM_GROUP = 16
QK_NOPE = 128
QK_ROPE = 64
V_HEAD = 128
ROPE_THETA = 10000.0
EPS = 1e-6
TOP_K = 2

LANES = 128
SUBLANES = 8
MXU_DIM = 256
VMEM_LIMIT = 56 * 1024 * 1024

SSM_CHUNK = 16
TOKEN_TILE = 256
ATTN_TQ = 512
ATTN_TK = 512
MOE_ROWS = 256
DISPATCH_CHUNK = 4096
NEG_BIG = -1e30


def _dot(a, b):
    return jnp.dot(a, b, preferred_element_type=F32)


def _split_bf16(a):
    hi = a.astype(BF16)
    lo = (a - hi.astype(F32)).astype(BF16)
    return hi, lo


def _rms(x, g):
    return x * lax.rsqrt(jnp.mean(x * x, axis=-1, keepdims=True) + EPS) * g


def _params(semantics):
    return pltpu.CompilerParams(dimension_semantics=semantics, vmem_limit_bytes=VMEM_LIMIT)


def _const_spec(shape):
    nd = len(shape)
    return pl.BlockSpec(shape, lambda *_: (0,) * nd, pipeline_mode=pl.Buffered(1))


def _ada_body(c_ref, w_ref, b_ref, o_ref):
    ch, cl = _split_bf16(c_ref[...])
    wh, wl = _split_bf16(w_ref[0])
    o_ref[0] = _dot(ch, wh) + _dot(cl, wh) + _dot(ch, wl) + b_ref[0]


def _ada_mod(c, w_ada, b_ada):
    depth, d, n6 = w_ada.shape
    b = c.shape[0]
    tn = 1024
    c_pad = jnp.zeros((SUBLANES, d), F32).at[:b].set(c)
    out = pl.pallas_call(
        _ada_body,
        grid=(depth, n6 // tn),
        in_specs=[pl.BlockSpec((SUBLANES, d), lambda l, j: (0, 0)),
                  pl.BlockSpec((1, d, tn), lambda l, j: (l, 0, j)),
                  pl.BlockSpec((1, 1, tn), lambda l, j: (l, 0, j))],
        out_specs=pl.BlockSpec((1, SUBLANES, tn), lambda l, j: (l, 0, j)),
        out_shape=jax.ShapeDtypeStruct((depth, SUBLANES, n6), F32),
        compiler_params=_params(("arbitrary", "arbitrary")),
        name="ada_mod",
    )(c_pad, w_ada, b_ada.reshape(depth, 1, n6))
    return out[:, :b].reshape(depth, b, 6, d)


def _rope_body(pos_ref, invf_ref, cos_ref, sin_ref):
    ang = pos_ref[...] * invf_ref[...]
    cos_ref[...] = jnp.cos(ang)
    sin_ref[...] = jnp.sin(ang)


def _rope_tables(positions):
    n = positions.size
    half = QK_ROPE // 2
    per_row = LANES // half
    inv_freq = jnp.power(ROPE_THETA, -jnp.arange(0, QK_ROPE, 2, dtype=F32) / QK_ROPE)
    pos = jnp.repeat(positions.reshape(n // per_row, per_row).astype(F32), half, axis=1)
    invf = jnp.tile(inv_freq, per_row).reshape(1, LANES)
    rows = n // per_row
    tr = min(rows, 1024)
    cos, sin = pl.pallas_call(
        _rope_body,
        grid=(rows // tr,),
        in_specs=[pl.BlockSpec((tr, LANES), lambda i: (i, 0)), pl.BlockSpec((1, LANES), lambda i: (0, 0))],
        out_specs=[pl.BlockSpec((tr, LANES), lambda i: (i, 0))] * 2,
        out_shape=[jax.ShapeDtypeStruct((rows, LANES), F32)] * 2,
        compiler_params=_params(("arbitrary",)),
        name="rope_tables",
    )(pos, invf)
    z = jnp.zeros((n, LANES - QK_ROPE), F32)
    cos, sin = cos.reshape(n, half), sin.reshape(n, half)
    return jnp.concatenate([cos, cos, z], axis=1), jnp.concatenate([sin, sin, z], axis=1)


def _mix_in_body(x_ref, mod_ref, g1_ref, win_ref, poolw_ref, pscale_ref, gnp_ref, qg_ref, kvg_ref,
                 wuq_ref, wuqr_ref, wukv_ref, cs_ref, sn_ref,
                 ypool_ref, ussm_ref, q_ref, k_ref, v_ref, ext_ref, *, tiles_per_batch, tm, heads, scale):
    tin = pl.program_id(0) % tiles_per_batch
    mod = mod_ref[0]
    sh_a, sc_a = mod[0:1], mod[1:2]
    h = _rms(x_ref[...], g1_ref[...]) * (1.0 + sc_a) + sh_a
    z = _dot(h.astype(BF16), win_ref[...])

    halo = max(POOL_WINDOWS)
    pw = len(POOL_WINDOWS) * LANES

    @pl.when(tin == 0)
    def _():
        ext_ref[0:halo, :] = jnp.zeros((halo, pw), F32)

    zp = z[:, 0:pw]
    ext_ref[halo:halo + tm, :] = zp
    t = tin * tm + lax.broadcasted_iota(jnp.int32, (tm, 1), 0)
    ys = []
    for gi, w in enumerate(POOL_WINDOWS):
        cols = slice(gi * LANES, (gi + 1) * LANES)
        tok = zp[:, cols]
        s = tok
        for j in range(1, w):
            s = s + ext_ref[halo - j:halo - j + tm, cols]
        cnt = jnp.minimum(t + 1, w).astype(F32)
        ys.append(_dot((s / cnt - tok).astype(BF16), poolw_ref[gi]))
    ypool = jnp.concatenate(ys, axis=1) * pscale_ref[...]
    ypool_ref[...] = _rms(ypool, gnp_ref[...]).astype(BF16)
    ext_ref[0:halo, :] = ext_ref[tm:tm + halo, :]

    ussm_ref[...] = z[:, pw:2 * pw]

    cs, sn = cs_ref[...], sn_ref[...]
    qn = _rms(z[:, 1024:1536], qg_ref[...]).astype(BF16)
    qm = _dot(qn, wuq_ref[...])
    qr = _dot(qn, wuqr_ref[...])
    cs_q, sn_q = cs * scale, sn * scale
    for hh in range(heads):
        o = hh * 2 * LANES
        q_ref[hh, :, 0:LANES] = (qm[:, o:o + LANES] * scale).astype(BF16)
        q_ref[hh, :, LANES:2 * LANES] = (qm[:, o + LANES:o + 2 * LANES] * cs_q
                                         + qr[:, hh * LANES:(hh + 1) * LANES] * sn_q).astype(BF16)
    kvn = _rms(z[:, 1536:1792], kvg_ref[...]).astype(BF16)
    kv = _dot(kvn, wukv_ref[...])
    kpe = (z[:, 1792:1920] * cs + z[:, 1920:2048] * sn).astype(BF16)
    for hh in range(heads):
        k_ref[hh, :, 0:LANES] = kv[:, hh * LANES:(hh + 1) * LANES].astype(BF16)
        k_ref[hh, :, LANES:2 * LANES] = kpe
    v_ref[...] = kv[:, heads * LANES:].astype(BF16)


def _mix_in(x, mod, g1, win, poolw, pscale, gnp, qg, kvg, wuq, wuqr, wukv, cs, sn, *, seq, heads):
    n, d = x.shape
    tm = TOKEN_TILE
    tpb = seq // tm
    pw = len(POOL_WINDOWS) * LANES
    halo = max(POOL_WINDOWS)
    scale = float((QK_NOPE + QK_ROPE) ** -0.5)
    row = lambda w: pl.BlockSpec((tm, w), lambda i: (i, 0))
    hd = pl.BlockSpec((heads, tm, 2 * LANES), lambda i: (0, i, 0))
    return pl.pallas_call(
        functools.partial(_mix_in_body, tiles_per_batch=tpb, tm=tm, heads=heads, scale=scale),
        grid=(n // tm,),
        in_specs=[row(d),
                  pl.BlockSpec((1, 6, d), lambda i: (i // tpb, 0, 0)),
                  _const_spec(g1.shape), _const_spec(win.shape), _const_spec(poolw.shape),
                  _const_spec(pscale.shape), _const_spec(gnp.shape), _const_spec(qg.shape),
                  _const_spec(kvg.shape), _const_spec(wuq.shape), _const_spec(wuqr.shape),
                  _const_spec(wukv.shape), row(LANES), row(LANES)],
        out_specs=[row(pw), row(pw), hd, hd, row(heads * V_HEAD)],
        out_shape=[jax.ShapeDtypeStruct((n, pw), BF16), jax.ShapeDtypeStruct((n, pw), F32),
                   jax.ShapeDtypeStruct((heads, n, 2 * LANES), BF16),
                   jax.ShapeDtypeStruct((heads, n, 2 * LANES), BF16),
                   jax.ShapeDtypeStruct((n, heads * V_HEAD), BF16)],
        scratch_shapes=[pltpu.VMEM((tm + halo, pw), F32)],
        compiler_params=_params(("arbitrary",)),
        name="mix_in",
    )(x, mod, g1, win, poolw, pscale, gnp, qg, kvg, wuq, wuqr, wukv, cs, sn)


def _s5_body(u_ref, m_ref, bc_ref, cc_ref, tab_ref, y_ref, sre_ref, sim_ref, *, rows):
    u = u_ref[0]
    half = SSM_CHUNK * SSM_GROUP
    y_ref[0, :, 0:half] = _dot(u[:, 0:half], m_ref[0])
    y_ref[0, :, half:2 * half] = _dot(u[:, half:2 * half], m_ref[1])
    x = _dot(u, bc_ref[0])
    sre_ref[...] = x[:, 0:LANES]
    sim_ref[...] = x[:, LANES:2 * LANES]
    tab = tab_ref[0]
    first_row = lax.broadcasted_iota(jnp.int32, (SUBLANES, LANES), 0) == 0

    def block(b, carry):
        cre, cim = carry
        r0 = pl.multiple_of(b * SUBLANES, SUBLANES)
        re = sre_ref[pl.ds(r0, SUBLANES), :]
        im = sim_ref[pl.ds(r0, SUBLANES), :]
        for n, k in enumerate((1, 2, 4)):
            tr, ti = tab[2 * n], tab[2 * n + 1]
            pre, pim = pltpu.roll(re, k, 0), pltpu.roll(im, k, 0)
            re, im = re + tr * pre - ti * pim, im + tr * pim + ti * pre
        pr, pi = tab[6], tab[7]
        ore = re + pr * cre - pi * cim
        oim = im + pr * cim + pi * cre
        sre_ref[pl.ds(r0, SUBLANES), :] = jnp.where(first_row, cre, pltpu.roll(ore, 1, 0))
        sim_ref[pl.ds(r0, SUBLANES), :] = jnp.where(first_row, cim, pltpu.roll(oim, 1, 0))
        last = SUBLANES - 1
        return (jnp.broadcast_to(ore[last:last + 1, :], (SUBLANES, LANES)),
                jnp.broadcast_to(oim[last:last + 1, :], (SUBLANES, LANES)))

    zero = jnp.zeros((SUBLANES, LANES), F32)
    lax.fori_loop(0, rows // SUBLANES, block, (zero, zero))
    sp = jnp.concatenate([sre_ref[...], sim_ref[...]], axis=1).astype(BF16)
    y_ref[0] = y_ref[0] + _dot(sp, cc_ref[0])


def _s5_scan(u_pairs, m, bc, cc, tab, *, batch):
    pairs, nchunks, width = u_pairs.shape
    rows = nchunks // batch
    half = SSM_CHUNK * SSM_GROUP
    return pl.pallas_call(
        functools.partial(_s5_body, rows=rows),
        grid=(pairs, batch),
        in_specs=[pl.BlockSpec((1, rows, width), lambda p, b: (p, b, 0)),
                  pl.BlockSpec((2, half, half), lambda p, b: (p, 0, 0)),
                  pl.BlockSpec((1, width, 2 * LANES), lambda p, b: (p, 0, 0)),
                  pl.BlockSpec((1, 2 * LANES, width), lambda p, b: (p, 0, 0)),
                  pl.BlockSpec((1, 8, SUBLANES, LANES), lambda p, b: (p, 0, 0, 0))],
        out_specs=pl.BlockSpec((1, rows, width), lambda p, b: (p, b, 0)),
        out_shape=jax.ShapeDtypeStruct((pairs, nchunks, width), F32),
        scratch_shapes=[pltpu.VMEM((rows, LANES), F32), pltpu.VMEM((rows, LANES), F32)],
        compiler_params=_params(("arbitrary", "arbitrary")),
        name="s5_scan",
    )(u_pairs, m, bc, cc, tab)


def _ssm_tables(lam_re, lam_im, log_dt, b_re, b_im, c_re, c_im):
    hp = lax.Precision.HIGHEST
    g, p = lam_re.shape
    hh = b_re.shape[-1]
    t = SSM_CHUNK
    pairs = g // 2
    dt = jnp.exp(log_dt)[:, None]
    mag = jnp.exp(lam_re * dt)
    ar, ai = mag * jnp.cos(lam_im * dt), mag * jnp.sin(lam_im * dt)
    den = lam_re * lam_re + lam_im * lam_im
    fr = ((ar - 1.0) * lam_re + ai * lam_im) / den
    fi = (ai * lam_re - (ar - 1.0) * lam_im) / den
    bbr = fr[..., None] * b_re - fi[..., None] * b_im
    bbi = fr[..., None] * b_im + fi[..., None] * b_re

    def powers(xr, xi, count):
        prs, pis = [jnp.ones_like(xr)], [jnp.zeros_like(xi)]
        for _ in range(count):
            prs.append(prs[-1] * xr - pis[-1] * xi)
            pis.append(prs[-2] * xi + pis[-1] * xr)
        return jnp.stack(prs), jnp.stack(pis)

    pr, pi = powers(ar, ai, t)
    car = c_re[None] * pr[:, :, None, :] - c_im[None] * pi[:, :, None, :]
    cai = c_im[None] * pr[:, :, None, :] + c_re[None] * pi[:, :, None, :]
    kern = (jnp.einsum('kgop,gpi->gkio', car[:t], bbr, precision=hp)
            - jnp.einsum('kgop,gpi->gkio', cai[:t], bbi, precision=hp))
    lag = jnp.arange(t)[None, :] - jnp.arange(t)[:, None]
    toep = jnp.where((lag >= 0)[None, :, :, None, None], kern[:, jnp.clip(lag, 0)], 0.0)
    m = toep.transpose(0, 1, 3, 2, 4).reshape(g, t * hh, t * hh)
    rev = jnp.arange(t - 1, -1, -1)
    bcr = (pr[rev][..., None] * bbr[None] - pi[rev][..., None] * bbi[None]).transpose(1, 0, 3, 2)
    bci = (pr[rev][..., None] * bbi[None] + pi[rev][..., None] * bbr[None]).transpose(1, 0, 3, 2)
    bri = jnp.stack([bcr, bci], axis=3).reshape(pairs, 2, t * hh, 2, p)
    eye = jnp.eye(2, dtype=F32)
    bc = (bri[:, :, :, :, None, :] * eye[None, :, None, None, :, None]).reshape(pairs, 2 * t * hh, 4 * p)
    ccr = car[1:].transpose(1, 3, 0, 2)
    cci = -cai[1:].transpose(1, 3, 0, 2)
    cri = jnp.stack([ccr, cci], axis=1).reshape(pairs, 2, 2, p, t * hh)
    cc = (cri.transpose(0, 2, 1, 3, 4)[:, :, :, :, None, :] * eye[None, None, :, None, :, None]
          ).reshape(pairs, 4 * p, 2 * t * hh)
    a16r, a16i = pr[t].reshape(pairs, 2 * p), pi[t].reshape(pairs, 2 * p)
    qr, qi = powers(a16r, a16i, SUBLANES)
    sub = jnp.arange(SUBLANES)
    tabs = []
    for k in (1, 2, 4):
        keep = (sub >= k).astype(F32)[None, :, None]
        tabs += [qr[k][:, None, :] * keep, qi[k][:, None, :] * keep]
    tabs += [qr[1:].transpose(1, 0, 2), qi[1:].transpose(1, 0, 2)]
    tab = jnp.stack(tabs, axis=1)
    return m.astype(BF16), bc.astype(BF16), cc.astype(BF16), tab


def _attn_body(q_ref, k_ref, v_ref, o_ref, *, tq, tk):
    i = pl.program_id(1)
    q = q_ref[0]

    def step(j, carry, masked):
        m, l, acc = carry
        k0 = pl.multiple_of(j * tk, tk)
        kb = k_ref[0, pl.ds(k0, tk), :]
        vb = v_ref[pl.ds(k0, tk), :]
        s = lax.dot_general(q, kb, (((1,), (1,)), ((), ())), preferred_element_type=F32)
        if masked:
            row = i * tq + lax.broadcasted_iota(jnp.int32, (tq, tk), 0)
            col = j * tk + lax.broadcasted_iota(jnp.int32, (tq, tk), 1)
            s = jnp.where(col <= row, s, NEG_BIG)
        m_new = jnp.maximum(m, jnp.max(s, axis=-1, keepdims=True))
        p = jnp.exp(s - m_new)
        alpha = jnp.exp(m - m_new)
        l = alpha * l + jnp.sum(p, axis=-1, keepdims=True)
        acc = alpha * acc + _dot(p.astype(BF16), vb)
        return m_new, l, acc

    init = (jnp.full((tq, 1), NEG_BIG, F32), jnp.zeros((tq, 1), F32), jnp.zeros((tq, V_HEAD), F32))
    n_full = (i * tq) // tk
    carry = lax.fori_loop(0, n_full, lambda j, c: step(j, c, False), init)
    for d in range(tq // tk):
        carry = step(n_full + d, carry, True)
    _, l, acc = carry
    o_ref[...] = (acc / l).astype(BF16)


def _mla_attn(q, k, v, *, batch, seq, heads):
    tq, tk = ATTN_TQ, ATTN_TK
    nq = seq // tq
    n = batch * seq
    return pl.pallas_call(
        functools.partial(_attn_body, tq=tq, tk=tk),
        grid=(batch * heads, nq),
        in_specs=[pl.BlockSpec((1, tq, 2 * LANES), lambda bh, i: (bh % heads, (bh // heads) * nq + i, 0)),
                  pl.BlockSpec((1, seq, 2 * LANES), lambda bh, i: (bh % heads, bh // heads, 0)),
                  pl.BlockSpec((seq, V_HEAD), lambda bh, i: (bh // heads, bh % heads))],
        out_specs=pl.BlockSpec((tq, V_HEAD), lambda bh, i: ((bh // heads) * nq + i, bh % heads)),
        out_shape=jax.ShapeDtypeStruct((n, heads * V_HEAD), BF16),
        compiler_params=_params(("arbitrary", "arbitrary")),
        name="mla_attn",
    )(q, k, v)


def _gelu_tanh(x):
    return 0.5 * x * (1.0 + jnp.tanh(math.sqrt(2.0 / math.pi) * (x + 0.044715 * (x * x * x))))


def _mix_out_body(x_ref, yp_ref, ysc_ref, us_ref, ym_ref, mod_ref, dsk_ref, wglu_ref, gns_ref, gnm_ref,
                  wo_ref, g2_ref, wrh_ref, wrl_ref, br_ref,
                  x1_ref, h2_ref, ridx_ref, rw_ref, *, n_exp, n_grp):
    mod = mod_ref[0]
    g_a, sh_f, sc_f = mod[2:3], mod[3:4], mod[4:5]
    ys = _gelu_tanh(ysc_ref[...] + dsk_ref[...] * us_ref[...])
    ys = ys * jax.nn.sigmoid(_dot(ys.astype(BF16), wglu_ref[...]))
    ysn = _rms(ys, gns_ref[...]).astype(BF16)
    ymn = _rms(ym_ref[...].astype(F32), gnm_ref[...]).astype(BF16)
    wp, ws = yp_ref.shape[1], ysc_ref.shape[1]
    o = (_dot(yp_ref[...], wo_ref[0:wp, :]) + _dot(ysn, wo_ref[wp:wp + ws, :])
         + _dot(ymn, wo_ref[wp + ws:, :]))
    x1 = x_ref[...] + g_a * o
    x1_ref[...] = x1
    h2 = _rms(x1, g2_ref[...]) * (1.0 + sc_f) + sh_f
    h2_ref[...] = h2

    hh, hl = _split_bf16(h2)
    logits = _dot(hh, wrh_ref[...]) + _dot(hl, wrh_ref[...]) + _dot(hh, wrl_ref[...]) + br_ref[...]
    lane = lax.broadcasted_iota(jnp.int32, logits.shape, 1).astype(F32)
    far = float(2 * LANES)
    per = n_exp // n_grp
    gl = jnp.where(lane >= n_exp, jnp.where(lane < n_exp + n_grp, logits, NEG_BIG), NEG_BIG)
    gmax = jnp.max(gl, axis=-1, keepdims=True)
    gtop = jnp.min(jnp.where(gl == gmax, lane, far), axis=-1, keepdims=True) - n_exp
    g_w = 1.0 / jnp.sum(jnp.exp(gl - gmax), axis=-1, keepdims=True)
    lo = gtop * per
    el = jnp.where(lane >= lo, jnp.where(lane < lo + per, logits, NEG_BIG), NEG_BIG)
    m1 = jnp.max(el, axis=-1, keepdims=True)
    i1 = jnp.min(jnp.where(el == m1, lane, far), axis=-1, keepdims=True)
    el2 = jnp.where(lane == i1, NEG_BIG, el)
    m2 = jnp.max(el2, axis=-1, keepdims=True)
    i2 = jnp.min(jnp.where(el2 == m2, lane, far), axis=-1, keepdims=True)
    e21 = jnp.exp(m2 - m1)
    w1 = g_w / (1.0 + e21)
    w2 = g_w * e21 / (1.0 + e21)
    ridx_ref[...] = jnp.where(lane == 0.0, i1, jnp.where(lane == 1.0, i2, 0.0)).astype(jnp.int32)
    rw_ref[...] = jnp.where(lane == 0.0, w1, jnp.where(lane == 1.0, w2, 0.0))


def _mix_out(x, ypool, yscan, ussm, ymla, mod, dsk, wglu, gns, gnm, wo, g2, wrh, wrl, br, *, seq, n_exp, n_grp):
    n, d = x.shape
    tm = TOKEN_TILE
    tpb = seq // tm
    row = lambda w: pl.BlockSpec((tm, w), lambda i: (i, 0))
    return pl.pallas_call(
        functools.partial(_mix_out_body, n_exp=n_exp, n_grp=n_grp),
        grid=(n // tm,),
        in_specs=[row(d), row(ypool.shape[1]), row(yscan.shape[1]), row(ussm.shape[1]), row(ymla.shape[1]),
                  pl.BlockSpec((1, 6, d), lambda i: (i // tpb, 0, 0)),
                  _const_spec(dsk.shape), _const_spec(wglu.shape), _const_spec(gns.shape), _const_spec(gnm.shape),
                  _const_spec(wo.shape), _const_spec(g2.shape), _const_spec(wrh.shape), _const_spec(wrl.shape),
                  _const_spec(br.shape)],
        out_specs=[row(d), row(d), row(LANES), row(LANES)],
        out_shape=[jax.ShapeDtypeStruct((n, d), F32), jax.ShapeDtypeStruct((n, d), F32),
                   jax.ShapeDtypeStruct((n, LANES), jnp.int32), jax.ShapeDtypeStruct((n, LANES), F32)],
        compiler_params=_params(("arbitrary",)),
        name="mix_out",
    )(x, ypool, yscan, ussm, ymla, mod, dsk, wglu, gns, gnm, wo, g2, wrh, wrl, br)


def _route_plan(ridx, n_exp, blk):
    expert = ridx[:, :TOP_K].reshape(-1)
    a = expert.shape[0]
    onehot = (expert[:, None] == jnp.arange(n_exp, dtype=jnp.int32)[None, :]).astype(jnp.int32)
    csum = jnp.cumsum(onehot, axis=0)
    counts = csum[-1]
    padded = (counts + blk - 1) // blk * blk
    pad_end = jnp.cumsum(padded)
    pad_start = pad_end - padded
    dest = jnp.sum(onehot * (csum - 1 + pad_start[None, :]), axis=1).astype(jnp.int32)
    nb = a // blk + n_exp
    bstart = jnp.arange(nb, dtype=jnp.int32) * blk
    be = jnp.minimum(jnp.searchsorted(pad_end, bstart, side='right'), n_exp - 1).astype(jnp.int32)
    nv = jnp.clip(counts[be] - (bstart - pad_start[be]), 0, blk).astype(jnp.int32)
    return dest, be, nv, nb


def _dispatch_body(dest_ref, h_hbm, xs_hbm, sem, *, chunk):
    base = pl.program_id(0) * chunk

    def issue(r, carry):
        tok = lax.shift_right_logical(base + r, 1)
        pltpu.make_async_copy(h_hbm.at[pl.ds(tok, 1)], xs_hbm.at[pl.ds(dest_ref[r], 1)], sem).start()
        return carry

    lax.fori_loop(0, chunk, issue, 0)
    pltpu.make_async_copy(h_hbm.at[pl.ds(0, chunk)], xs_hbm.at[pl.ds(0, chunk)], sem).wait()


def _dispatch(dest, h2, n_rows):
    a = dest.shape[0]
    d = h2.shape[1]
    chunk = min(DISPATCH_CHUNK, a)
    return pl.pallas_call(
        functools.partial(_dispatch_body, chunk=chunk),
        grid=(a // chunk,),
        in_specs=[pl.BlockSpec((chunk,), lambda i: (i,), memory_space=pltpu.SMEM),
                  pl.BlockSpec(memory_space=pl.ANY)],
        out_specs=pl.BlockSpec(memory_space=pl.ANY),
        out_shape=jax.ShapeDtypeStruct((n_rows, d), h2.dtype),
        scratch_shapes=[pltpu.SemaphoreType.DMA(())],
        compiler_params=pltpu.CompilerParams(dimension_semantics=("arbitrary",), has_side_effects=True),
        name="moe_dispatch",
    )(dest, h2)


def _expert_body(be_ref, nv_ref, xs_ref, w1_ref, w3_ref, w2_ref, ys_ref, w1b, w3b, w2b, *, blk):
    j = pl.program_id(0)
    e = be_ref[j]
    prev = be_ref[jnp.maximum(j - 1, 0)]

    @pl.when(jnp.logical_or(j == 0, e != prev))
    def _():
        w1b[...] = w1_ref[0].astype(BF16)
        w3b[...] = w3_ref[0].astype(BF16)
        w2b[...] = w2_ref[0].astype(BF16)

    nv = nv_ref[j]

    @pl.when(nv > 0)
    def _():
        row = lax.broadcasted_iota(jnp.int32, (blk, 1), 0)
        x = jnp.where(row < nv, xs_ref[...], 0.0).astype(BF16)
        a = _dot(x, w1b[...])
        b = _dot(x, w3b[...])
        ys_ref[...] = _dot((a * jax.nn.sigmoid(a) * b).astype(BF16), w2b[...])

    @pl.when(nv == 0)
    def _():
        ys_ref[...] = jnp.zeros(ys_ref.shape, F32)


def _experts(be, nv, xs, w1, w3, w2, nb):
    blk = MOE_ROWS
    _, d, de = w1.shape
    grid_spec = pltpu.PrefetchScalarGridSpec(
        num_scalar_prefetch=2,
        grid=(nb,),
        in_specs=[pl.BlockSpec((blk, d), lambda j, be, nv: (j, 0)),
                  pl.BlockSpec((1, d, de), lambda j, be, nv: (be[j], 0, 0)),
                  pl.BlockSpec((1, d, de), lambda j, be, nv: (be[j], 0, 0)),
                  pl.BlockSpec((1, de, d), lambda j, be, nv: (be[j], 0, 0))],
        out_specs=pl.BlockSpec((blk, d), lambda j, be, nv: (j, 0)),
        scratch_shapes=[pltpu.VMEM((d, de), BF16), pltpu.VMEM((d, de), BF16), pltpu.VMEM((de, d), BF16)],
    )
    return pl.pallas_call(
        functools.partial(_expert_body, blk=blk),
        grid_spec=grid_spec,
        out_shape=jax.ShapeDtypeStruct((nb * blk, d), F32),
        compiler_params=_params(("arbitrary",)),
        name="moe_experts",
    )(be, nv, xs, w1, w3, w2)


def _combine_body(dest_ref, x1_ref, rw_ref, mod_ref, fg_ref, ys_hbm, out_ref, buf, sem, *, tm, final):
    def issue(r, carry):
        for k in range(TOP_K):
            pltpu.make_async_copy(ys_hbm.at[pl.ds(dest_ref[TOP_K * r + k], 1)], buf.at[k, pl.ds(r, 1)], sem).start()
        return carry

    lax.fori_loop(0, tm, issue, 0)
    for k in range(TOP_K):
        pltpu.make_async_copy(ys_hbm.at[pl.ds(0, tm)], buf.at[k], sem).wait()
    w = rw_ref[...]
    y = buf[0] * w[:, 0:1] + buf[1] * w[:, 1:2]
    x2 = x1_ref[...] + mod_ref[0][5:6] * y
    out_ref[...] = _rms(x2, fg_ref[...]) if final else x2


def _combine(dest, x1, rw, mod, fg, ys, *, seq, final):
    n, d = x1.shape
    tm = TOKEN_TILE
    tpb = seq // tm
    row = lambda w: pl.BlockSpec((tm, w), lambda i: (i, 0))
    return pl.pallas_call(
        functools.partial(_combine_body, tm=tm, final=final),
        grid=(n // tm,),
        in_specs=[pl.BlockSpec((TOP_K * tm,), lambda i: (i,), memory_space=pltpu.SMEM),
                  row(d), row(LANES),
                  pl.BlockSpec((1, 6, d), lambda i: (i // tpb, 0, 0)),
                  pl.BlockSpec((1, d), lambda i: (0, 0)),
                  pl.BlockSpec(memory_space=pl.ANY)],
        out_specs=row(d),
        out_shape=jax.ShapeDtypeStruct((n, d), F32),
        scratch_shapes=[pltpu.VMEM((TOP_K, tm, d), F32), pltpu.SemaphoreType.DMA(())],
        compiler_params=_params(("arbitrary",)),
        name="moe_combine",
    )(dest, x1, rw, mod, fg, ys)


def _rot_half_cols(w):
    half = w.shape[-1] // 2
    return jnp.concatenate([-w[..., half:], w[..., :half]], axis=-1)


def _layer_weights(l, w_in, pool_w, w_uq, w_ukv, w_out, ssm_w_glu, router_w_group, router_b_group,
                   router_w_expert, router_b_expert):
    d = w_in.shape[1]
    heads = w_uq.shape[2]
    wi = w_in[l]
    kpe = wi[:, -QK_ROPE:]
    pad = jnp.zeros((d, LANES - QK_ROPE), F32)
    win = jnp.concatenate([wi[:, :-QK_ROPE], kpe, pad, _rot_half_cols(kpe), pad], axis=1).astype(BF16)
    uq = w_uq[l]
    r = uq.shape[0]
    zq = jnp.zeros((r, heads, LANES - QK_ROPE), F32)
    wuq = jnp.concatenate([uq, zq], axis=-1).reshape(r, heads * 2 * LANES).astype(BF16)
    wuqr = jnp.concatenate([_rot_half_cols(uq[..., QK_NOPE:]), zq], axis=-1).reshape(r, heads * LANES).astype(BF16)
    ukv = w_ukv[l]
    wukv = jnp.concatenate([ukv[..., :QK_NOPE].reshape(ukv.shape[0], -1),
                            ukv[..., QK_NOPE:].reshape(ukv.shape[0], -1)], axis=1).astype(BF16)
    n_grp = router_w_group.shape[-1]
    n_exp = router_w_expert.shape[-1]
    wr = jnp.concatenate([router_w_expert[l], router_w_group[l],
                          jnp.zeros((d, LANES - n_exp - n_grp), F32)], axis=1)
    wrh, wrl = _split_bf16(wr)
    br = jnp.concatenate([router_b_expert[l], router_b_group[l],
                          jnp.zeros((LANES - n_exp - n_grp,), F32)]).reshape(1, LANES)
    return dict(win=win, poolw=pool_w[l].astype(BF16), wuq=wuq, wuqr=wuqr, wukv=wukv,
                wo=w_out[l].astype(BF16), wglu=ssm_w_glu[l].astype(BF16), wrh=wrh, wrl=wrl, br=br)


def kernel(x, c, positions, w_ada, b_ada, norm1_g, w_in, pool_w, pool_scale, ssm_lam_re, ssm_lam_im, ssm_log_dt, ssm_b_re, ssm_b_im, ssm_c_re, ssm_c_im, ssm_d, ssm_w_glu, q_norm_g, kv_norm_g, w_uq, w_ukv, out_norm_g, w_out, norm2_g, router_w_group, router_b_group, router_w_expert, router_b_expert, w_gate, w_up, w_down, final_g):
    batch, seq, d = x.shape
    depth = w_ada.shape[0]
    n = batch * seq
    heads = w_uq.shape[2]
    n_grp = router_w_group.shape[-1]
    n_exp = router_w_expert.shape[-1]
    pw = pool_w.shape[1] * pool_w.shape[2]
    sw = ssm_d.shape[-1]
    pairs = sw // (2 * SSM_GROUP)
    assert SSM_CHUNK * SSM_GROUP == MXU_DIM and pw == len(POOL_WINDOWS) * LANES
    assert seq % ATTN_TQ == 0 and seq % TOKEN_TILE == 0 and (n * TOP_K) % MOE_ROWS == 0

    mod_all = _ada_mod(c, w_ada, b_ada)
    cs, sn = _rope_tables(positions)
    xf = x.reshape(n, d)
    nchunks = n // SSM_CHUNK
    for l in range(depth):
        wts = _layer_weights(l, w_in, pool_w, w_uq, w_ukv, w_out, ssm_w_glu, router_w_group, router_b_group,
                             router_w_expert, router_b_expert)
        mod = mod_all[l]
        gn = out_norm_g[l]
        ypool, ussm, q, k, v = _mix_in(
            xf, mod, norm1_g[l].reshape(1, d), wts['win'], wts['poolw'], pool_scale[l].reshape(1, pw),
            gn[:pw].reshape(1, pw), q_norm_g[l].reshape(1, -1), kv_norm_g[l].reshape(1, -1),
            wts['wuq'], wts['wuqr'], wts['wukv'], cs, sn, seq=seq, heads=heads)
        m, bc, cc, tab = _ssm_tables(ssm_lam_re[l], ssm_lam_im[l], ssm_log_dt[l], ssm_b_re[l], ssm_b_im[l],
                                     ssm_c_re[l], ssm_c_im[l])
        u_pairs = (ussm.astype(BF16).reshape(nchunks, SSM_CHUNK, pairs, 2, SSM_GROUP)
                   .transpose(2, 0, 3, 1, 4).reshape(pairs, nchunks, 2 * MXU_DIM))
        y_pairs = _s5_scan(u_pairs, m, bc, cc, tab, batch=batch)
        yscan = (y_pairs.reshape(pairs, nchunks, 2, SSM_CHUNK, SSM_GROUP)
                 .transpose(1, 3, 0, 2, 4).reshape(n, sw))
        ymla = _mla_attn(q, k, v, batch=batch, seq=seq, heads=heads)
        x1, h2, ridx, rw = _mix_out(
            xf, ypool, yscan, ussm, ymla, mod, ssm_d[l].reshape(1, sw), wts['wglu'],
            gn[pw:pw + sw].reshape(1, sw), gn[pw + sw:].reshape(1, -1), wts['wo'], norm2_g[l].reshape(1, d),
            wts['wrh'], wts['wrl'], wts['br'], seq=seq, n_exp=n_exp, n_grp=n_grp)
        dest, be, nv, nb = _route_plan(ridx, n_exp, MOE_ROWS)
        xs = _dispatch(dest, h2, nb * MOE_ROWS)
        ys = _experts(be, nv, xs, w_gate[l], w_up[l], w_down[l], nb)
        xf = _combine(dest, x1, rw, mod, final_g.reshape(1, d), ys, seq=seq, final=(l == depth - 1))
    return xf.reshape(batch, seq, d)
```

```python
import functools
import math

import jax
import jax.numpy as jnp
from jax import lax
from jax.experimental import pallas as pl
from jax.experimental.pallas import tpu as pltpu

F32 = jnp.float32
BF16 = jnp.bfloat16

POOL_WINDOWS = (2, 4, 8, 16)
SSM_GROUP = 16
QK_NOPE = 128
QK_ROPE = 64
V_HEAD = 128
ROPE_THETA = 10000.0
EPS = 1e-6
TOP_K = 2

LANES = 128
SUBLANES = 8
MXU_DIM = 256
VMEM_LIMIT = 56 * 1024 * 1024

SSM_CHUNK = 16
S5_ROWS = 4096
TOKEN_TILE = 256
ATTN_TQ = 512
ATTN_TK = 512
MOE_ROWS = 256
NEG_BIG = -1e30


def _dot(a, b):
    return jnp.dot(a, b, preferred_element_type=F32)


def _split_bf16(a):
    hi = a.astype(BF16)
    lo = (a - hi.astype(F32)).astype(BF16)
    return hi, lo


def _rms(x, g):
    return x * lax.rsqrt(jnp.mean(x * x, axis=-1, keepdims=True) + EPS) * g


def _params(semantics):
    return pltpu.CompilerParams(dimension_semantics=semantics, vmem_limit_bytes=VMEM_LIMIT)


def _const_spec(shape):
    nd = len(shape)
    return pl.BlockSpec(shape, lambda *_: (0,) * nd, pipeline_mode=pl.Buffered(1))


def _ada_body(c_ref, w_ref, b_ref, o_ref):
    ch, cl = _split_bf16(c_ref[...])
    wh, wl = _split_bf16(w_ref[0])
    o_ref[0] = _dot(ch, wh) + _dot(cl, wh) + _dot(ch, wl) + b_ref[0]


def _ada_mod(c, w_ada, b_ada):
    depth, d, n6 = w_ada.shape
    b = c.shape[0]
    tn = 1024
    c_pad = jnp.zeros((SUBLANES, d), F32).at[:b].set(c)
    out = pl.pallas_call(
        _ada_body,
        grid=(depth, n6 // tn),
        in_specs=[pl.BlockSpec((SUBLANES, d), lambda l, j: (0, 0)),
                  pl.BlockSpec((1, d, tn), lambda l, j: (l, 0, j)),
                  pl.BlockSpec((1, 1, tn), lambda l, j: (l, 0, j))],
        out_specs=pl.BlockSpec((1, SUBLANES, tn), lambda l, j: (l, 0, j)),
        out_shape=jax.ShapeDtypeStruct((depth, SUBLANES, n6), F32),
        compiler_params=_params(("arbitrary", "arbitrary")),
        name="ada_mod",
    )(c_pad, w_ada, b_ada.reshape(depth, 1, n6))
    return out[:, :b].reshape(depth, b, 6, d)


def _rope_body(pos_ref, invf_ref, cos_ref, sin_ref):
    ang = pos_ref[...] * invf_ref[...]
    cos_ref[...] = jnp.cos(ang)
    sin_ref[...] = jnp.sin(ang)


def _rope_tables(positions):
    n = positions.size
    half = QK_ROPE // 2
    per_row = LANES // half
    inv_freq = jnp.power(ROPE_THETA, -jnp.arange(0, QK_ROPE, 2, dtype=F32) / QK_ROPE)
    pos = jnp.repeat(positions.reshape(n // per_row, per_row).astype(F32), half, axis=1)
    invf = jnp.tile(inv_freq, per_row).reshape(1, LANES)
    rows = n // per_row
    tr = min(rows, 1024)
    cos, sin = pl.pallas_call(
        _rope_body,
        grid=(rows // tr,),
        in_specs=[pl.BlockSpec((tr, LANES), lambda i: (i, 0)), pl.BlockSpec((1, LANES), lambda i: (0, 0))],
        out_specs=[pl.BlockSpec((tr, LANES), lambda i: (i, 0))] * 2,
        out_shape=[jax.ShapeDtypeStruct((rows, LANES), F32)] * 2,
        compiler_params=_params(("arbitrary",)),
        name="rope_tables",
    )(pos, invf)
    z = jnp.zeros((n, LANES - QK_ROPE), F32)
    cos, sin = cos.reshape(n, half), sin.reshape(n, half)
    return jnp.concatenate([cos, cos, z], axis=1), jnp.concatenate([sin, sin, z], axis=1)


def _mix_in_body(x_ref, mod_ref, g1_ref, win_ref, poolw_ref, pscale_ref, gnp_ref, qg_ref, kvg_ref,
                 wuq_ref, wuqr_ref, wukv_ref, cs_ref, sn_ref,
                 ypool_ref, ussm_ref, q_ref, k_ref, v_ref, ext_ref, *, tiles_per_batch, tm, heads, scale):
    tin = pl.program_id(0) % tiles_per_batch
    mod = mod_ref[0]
    sh_a, sc_a = mod[0:1], mod[1:2]
    h = _rms(x_ref[...], g1_ref[...]) * (1.0 + sc_a) + sh_a
    z = _dot(h.astype(BF16), win_ref[...])

    halo = max(POOL_WINDOWS)
    pw = len(POOL_WINDOWS) * LANES

    @pl.when(tin == 0)
    def _():
        ext_ref[0:halo, :] = jnp.zeros((halo, pw), F32)

    zp = z[:, 0:pw]
    ext_ref[halo:halo + tm, :] = zp
    t = tin * tm + lax.broadcasted_iota(jnp.int32, (tm, 1), 0)
    ys = []
    for gi, w in enumerate(POOL_WINDOWS):
        cols = slice(gi * LANES, (gi + 1) * LANES)
        tok = zp[:, cols]
        s = tok
        for j in range(1, w):
            s = s + ext_ref[halo - j:halo - j + tm, cols]
        cnt = jnp.minimum(t + 1, w).astype(F32)
        ys.append(_dot((s / cnt - tok).astype(BF16), poolw_ref[gi]))
    ypool = jnp.concatenate(ys, axis=1) * pscale_ref[...]
    ypool_ref[...] = _rms(ypool, gnp_ref[...]).astype(BF16)
    ext_ref[0:halo, :] = ext_ref[tm:tm + halo, :]

    ussm_ref[...] = z[:, pw:2 * pw]

    cs, sn = cs_ref[...], sn_ref[...]
    qn = _rms(z[:, 1024:1536], qg_ref[...]).astype(BF16)
    qm = _dot(qn, wuq_ref[...])
    qr = _dot(qn, wuqr_ref[...])
    cs_q, sn_q = cs * scale, sn * scale
    for hh in range(heads):
        o = hh * 2 * LANES
        q_ref[hh, :, 0:LANES] = (qm[:, o:o + LANES] * scale).astype(BF16)
        q_ref[hh, :, LANES:2 * LANES] = (qm[:, o + LANES:o + 2 * LANES] * cs_q
                                         + qr[:, hh * LANES:(hh + 1) * LANES] * sn_q).astype(BF16)
    kvn = _rms(z[:, 1536:1792], kvg_ref[...]).astype(BF16)
    kv = _dot(kvn, wukv_ref[...])
    kpe = (z[:, 1792:1920] * cs + z[:, 1920:2048] * sn).astype(BF16)
    for hh in range(heads):
        k_ref[hh, :, 0:LANES] = kv[:, hh * LANES:(hh + 1) * LANES].astype(BF16)
        k_ref[hh, :, LANES:2 * LANES] = kpe
    v_ref[...] = kv[:, heads * LANES:].astype(BF16)


def _mix_in(x, mod, g1, win, poolw, pscale, gnp, qg, kvg, wuq, wuqr, wukv, cs, sn, *, seq, heads):
    n, d = x.shape
    tm = TOKEN_TILE
    tpb = seq // tm
    pw = len(POOL_WINDOWS) * LANES
    halo = max(POOL_WINDOWS)
    scale = float((QK_NOPE + QK_ROPE) ** -0.5)
    row = lambda w: pl.BlockSpec((tm, w), lambda i: (i, 0))
    hd = pl.BlockSpec((heads, tm, 2 * LANES), lambda i: (0, i, 0))
    return pl.pallas_call(
        functools.partial(_mix_in_body, tiles_per_batch=tpb, tm=tm, heads=heads, scale=scale),
        grid=(n // tm,),
        in_specs=[row(d),
                  pl.BlockSpec((1, 6, d), lambda i: (i // tpb, 0, 0)),
                  _const_spec(g1.shape), _const_spec(win.shape), _const_spec(poolw.shape),
                  _const_spec(pscale.shape), _const_spec(gnp.shape), _const_spec(qg.shape),
                  _const_spec(kvg.shape), _const_spec(wuq.shape), _const_spec(wuqr.shape),
                  _const_spec(wukv.shape), row(LANES), row(LANES)],
        out_specs=[row(pw), row(pw), hd, hd, row(heads * V_HEAD)],
        out_shape=[jax.ShapeDtypeStruct((n, pw), BF16), jax.ShapeDtypeStruct((n, pw), F32),
                   jax.ShapeDtypeStruct((heads, n, 2 * LANES), BF16),
                   jax.ShapeDtypeStruct((heads, n, 2 * LANES), BF16),
                   jax.ShapeDtypeStruct((n, heads * V_HEAD), BF16)],
        scratch_shapes=[pltpu.VMEM((tm + halo, pw), F32)],
        compiler_params=_params(("arbitrary",)),
        name="mix_in",
    )(x, mod, g1, win, poolw, pscale, gnp, qg, kvg, wuq, wuqr, wukv, cs, sn)


def _s5_body(u_ref, m_ref, bc_ref, cc_ref, tab_ref, y_ref, xcat_ref, sre_ref, sim_ref, car_ref, *, nc):
    t_ch = SSM_CHUNK
    sw = tab_ref.shape[-1]

    @pl.when(pl.program_id(2) == 0)
    def _():
        car_ref[...] = jnp.zeros(car_ref.shape, F32)

    for t in range(t_ch):
        xcat_ref[:, t * LANES:(t + 1) * LANES] = u_ref[pl.ds(t, nc, stride=t_ch), :].astype(BF16)
    xcat = xcat_ref[...]
    y_in = _dot(xcat, m_ref[0])
    x = _dot(xcat, bc_ref[0])
    sre_ref[...] = x[:, 0:sw]
    sim_ref[...] = x[:, sw:2 * sw]
    first_row = lax.broadcasted_iota(jnp.int32, (SUBLANES, sw), 0) == 0

    def block(b, carry):
        cre, cim = carry
        r0 = pl.multiple_of(b * SUBLANES, SUBLANES)
        re = sre_ref[pl.ds(r0, SUBLANES), :]
        im = sim_ref[pl.ds(r0, SUBLANES), :]
        for n, k in enumerate((1, 2, 4)):
            tr, ti = tab_ref[0, 2 * n], tab_ref[0, 2 * n + 1]
            pre, pim = pltpu.roll(re, k, 0), pltpu.roll(im, k, 0)
            re, im = re + tr * pre - ti * pim, im + tr * pim + ti * pre
        pr, pi = tab_ref[0, 6], tab_ref[0, 7]
        ore = re + pr * cre - pi * cim
        oim = im + pr * cim + pi * cre
        sre_ref[pl.ds(r0, SUBLANES), :] = jnp.where(first_row, cre, pltpu.roll(ore, 1, 0))
        sim_ref[pl.ds(r0, SUBLANES), :] = jnp.where(first_row, cim, pltpu.roll(oim, 1, 0))
        last = SUBLANES - 1
        return (jnp.broadcast_to(ore[last:last + 1, :], (SUBLANES, sw)),
                jnp.broadcast_to(oim[last:last + 1, :], (SUBLANES, sw)))

    cre, cim = lax.fori_loop(0, nc // SUBLANES, block, (car_ref[0], car_ref[1]))
    car_ref[0] = cre
    car_ref[1] = cim
    sp = jnp.concatenate([sre_ref[...], sim_ref[...]], axis=1).astype(BF16)
    y = y_in + _dot(sp, cc_ref[0])
    for t in range(t_ch):
        y_ref[pl.ds(t, nc, stride=t_ch), :] = y[:, t * LANES:(t + 1) * LANES]


def _s5_scan(u, m, bc, cc, tab, *, batch):
    n, width = u.shape
    tiles = width // LANES
    seq = n // batch
    rs = min(S5_ROWS, seq)
    steps = seq // rs
    nc = rs // SSM_CHUNK
    sw = tab.shape[-1]
    kw = SSM_CHUNK * LANES
    blk = pl.BlockSpec((rs, LANES), lambda j, b, r: (b * steps + r, j))
    return pl.pallas_call(
        functools.partial(_s5_body, nc=nc),
        grid=(tiles, batch, steps),
        in_specs=[blk,
                  pl.BlockSpec((1, kw, kw), lambda j, b, r: (j, 0, 0)),
                  pl.BlockSpec((1, kw, 2 * sw), lambda j, b, r: (j, 0, 0)),
                  pl.BlockSpec((1, 2 * sw, kw), lambda j, b, r: (j, 0, 0)),
                  pl.BlockSpec((1, 8, SUBLANES, sw), lambda j, b, r: (j, 0, 0, 0))],
        out_specs=blk,
        out_shape=jax.ShapeDtypeStruct((n, width), F32),
        scratch_shapes=[pltpu.VMEM((nc, kw), BF16), pltpu.VMEM((nc, sw), F32), pltpu.VMEM((nc, sw), F32),
                        pltpu.VMEM((2, SUBLANES, sw), F32)],
        compiler_params=_params(("arbitrary", "arbitrary", "arbitrary")),
        name="s5_scan",
    )(u, m, bc, cc, tab)


def _ssm_tables(lam_re, lam_im, log_dt, b_re, b_im, c_re, c_im):
    hp = lax.Precision.HIGHEST
    g, p = lam_re.shape
    hh = b_re.shape[-1]
    t = SSM_CHUNK
    gq = LANES // hh
    tiles = g // gq
    dt = jnp.exp(log_dt)[:, None]
    mag = jnp.exp(lam_re * dt)
    ar, ai = mag * jnp.cos(lam_im * dt), mag * jnp.sin(lam_im * dt)
    den = lam_re * lam_re + lam_im * lam_im
    fr = ((ar - 1.0) * lam_re + ai * lam_im) / den
    fi = (ai * lam_re - (ar - 1.0) * lam_im) / den
    bbr = fr[..., None] * b_re - fi[..., None] * b_im
    bbi = fr[..., None] * b_im + fi[..., None] * b_re

    def powers(xr, xi, count):
        prs, pis = [jnp.ones_like(xr)], [jnp.zeros_like(xi)]
        for _ in range(count):
            prs.append(prs[-1] * xr - pis[-1] * xi)
            pis.append(prs[-2] * xi + pis[-1] * xr)
        return jnp.stack(prs), jnp.stack(pis)

    pr, pi = powers(ar, ai, t)
    car = c_re[None] * pr[:, :, None, :] - c_im[None] * pi[:, :, None, :]
    cai = c_im[None] * pr[:, :, None, :] + c_re[None] * pi[:, :, None, :]
    kern = (jnp.einsum('kgop,gpi->gkio', car[:t], bbr, precision=hp)
            - jnp.einsum('kgop,gpi->gkio', cai[:t], bbi, precision=hp))
    lag = jnp.arange(t)[None, :] - jnp.arange(t)[:, None]
    toep = jnp.where((lag >= 0)[None, :, :, None, None], kern[:, jnp.clip(lag, 0)], 0.0)
    eye = jnp.eye(gq, dtype=F32)
    kw = t * gq * hh
    toep = toep.reshape(tiles, gq, t, t, hh, hh)
    m = (toep.transpose(0, 2, 1, 4, 3, 5)[:, :, :, :, :, None, :]
         * eye[None, None, :, None, None, :, None]).reshape(tiles, kw, kw)
    rev = jnp.arange(t - 1, -1, -1)
    bcr = (pr[rev][..., None] * bbr[None] - pi[rev][..., None] * bbi[None]).transpose(1, 0, 3, 2)
    bci = (pr[rev][..., None] * bbi[None] + pi[rev][..., None] * bbr[None]).transpose(1, 0, 3, 2)
    bri = jnp.stack([bcr, bci], axis=3).reshape(tiles, gq, t, hh, 2, p)
    bc = (bri.transpose(0, 2, 1, 3, 4, 5)[:, :, :, :, :, None, :]
          * eye[None, None, :, None, None, :, None]).reshape(tiles, kw, 2 * gq * p)
    ccr = car[1:].transpose(1, 3, 0, 2)
    cci = -cai[1:].transpose(1, 3, 0, 2)
    cri = jnp.stack([ccr, cci], axis=1).reshape(tiles, gq, 2, p, t, hh)
    cc = (cri.transpose(0, 2, 1, 3, 4, 5)[:, :, :, :, :, None, :]
          * eye[None, None, :, None, None, :, None]).reshape(tiles, 2 * gq * p, kw)
    a16r, a16i = pr[t].reshape(tiles, gq * p), pi[t].reshape(tiles, gq * p)
    qr, qi = powers(a16r, a16i, SUBLANES)
    sub = jnp.arange(SUBLANES)
    tabs = []
    for k in (1, 2, 4):
        keep = (sub >= k).astype(F32)[None, :, None]
        tabs += [qr[k][:, None, :] * keep, qi[k][:, None, :] * keep]
    tabs += [qr[1:].transpose(1, 0, 2), qi[1:].transpose(1, 0, 2)]
    tab = jnp.stack(tabs, axis=1)
    return m.astype(BF16), bc.astype(BF16), cc.astype(BF16), tab


def _attn_body(q_ref, k_ref, v_ref, o_ref, *, tq, tk):
    i = pl.program_id(1)
    q = q_ref[0]

    def step(j, carry, masked):
        m, l, acc = carry
        k0 = pl.multiple_of(j * tk, tk)
        kb = k_ref[0, pl.ds(k0, tk), :]
        vb = v_ref[pl.ds(k0, tk), :]
        s = lax.dot_general(q, kb, (((1,), (1,)), ((), ())), preferred_element_type=F32)
        if masked:
            row = i * tq + lax.broadcasted_iota(jnp.int32, (tq, tk), 0)
            col = j * tk + lax.broadcasted_iota(jnp.int32, (tq, tk), 1)
            s = jnp.where(col <= row, s, NEG_BIG)
        m_new = jnp.maximum(m, jnp.max(s, axis=-1, keepdims=True))
        p = jnp.exp(s - m_new)
        alpha = jnp.exp(m - m_new)
        l = alpha * l + jnp.sum(p, axis=-1, keepdims=True)
        acc = alpha * acc + _dot(p.astype(BF16), vb)
        return m_new, l, acc

    init = (jnp.full((tq, 1), NEG_BIG, F32), jnp.zeros((tq, 1), F32), jnp.zeros((tq, V_HEAD), F32))
    n_full = (i * tq) // tk
    carry = lax.fori_loop(0, n_full, lambda j, c: step(j, c, False), init)
    for d in range(tq // tk):
        carry = step(n_full + d, carry, True)
    _, l, acc = carry
    o_ref[...] = (acc / l).astype(BF16)


def _mla_attn(q, k, v, *, batch, seq, heads):
    tq, tk = ATTN_TQ, ATTN_TK
    nq = seq // tq
    n = batch * seq
    return pl.pallas_call(
        functools.partial(_attn_body, tq=tq, tk=tk),
        grid=(batch * heads, nq),
        in_specs=[pl.BlockSpec((1, tq, 2 * LANES), lambda bh, i: (bh % heads, (bh // heads) * nq + i, 0)),
                  pl.BlockSpec((1, seq, 2 * LANES), lambda bh, i: (bh % heads, bh // heads, 0)),
                  pl.BlockSpec((seq, V_HEAD), lambda bh, i: (bh // heads, bh % heads))],
        out_specs=pl.BlockSpec((tq, V_HEAD), lambda bh, i: ((bh // heads) * nq + i, bh % heads)),
        out_shape=jax.ShapeDtypeStruct((n, heads * V_HEAD), BF16),
        compiler_params=_params(("arbitrary", "arbitrary")),
        name="mla_attn",
    )(q, k, v)


def _gelu_tanh(x):
    return 0.5 * x * (1.0 + jnp.tanh(math.sqrt(2.0 / math.pi) * (x + 0.044715 * (x * x * x))))


def _mix_out_body(x_ref, yp_ref, ysc_ref, us_ref, ym_ref, mod_ref, dsk_ref, wglu_ref, gns_ref, gnm_ref,
                  wo_ref, g2_ref, wrh_ref, wrl_ref, br_ref,
                  x1_ref, h2_ref, ridx_ref, rw_ref, *, n_exp, n_grp):
    mod = mod_ref[0]
    g_a, sh_f, sc_f = mod[2:3], mod[3:4], mod[4:5]
    ys = _gelu_tanh(ysc_ref[...] + dsk_ref[...] * us_ref[...])
    ys = ys * jax.nn.sigmoid(_dot(ys.astype(BF16), wglu_ref[...]))
    ysn = _rms(ys, gns_ref[...]).astype(BF16)
    ymn = _rms(ym_ref[...].astype(F32), gnm_ref[...]).astype(BF16)
    wp, ws = yp_ref.shape[1], ysc_ref.shape[1]
    o = (_dot(yp_ref[...], wo_ref[0:wp, :]) + _dot(ysn, wo_ref[wp:wp + ws, :])
         + _dot(ymn, wo_ref[wp + ws:, :]))
    x1 = x_ref[...] + g_a * o
    x1_ref[...] = x1
    h2 = _rms(x1, g2_ref[...]) * (1.0 + sc_f) + sh_f
    h2_ref[...] = h2

    hh, hl = _split_bf16(h2)
    logits = _dot(hh, wrh_ref[...]) + _dot(hl, wrh_ref[...]) + _dot(hh, wrl_ref[...]) + br_ref[...]
    lane = lax.broadcasted_iota(jnp.int32, logits.shape, 1).astype(F32)
    far = float(2 * LANES)
    per = n_exp // n_grp
    gl = jnp.where(lane >= n_exp, jnp.where(lane < n_exp + n_grp, logits, NEG_BIG), NEG_BIG)
    gmax = jnp.max(gl, axis=-1, keepdims=True)
    gtop = jnp.min(jnp.where(gl == gmax, lane, far), axis=-1, keepdims=True) - n_exp
    g_w = 1.0 / jnp.sum(jnp.exp(gl - gmax), axis=-1, keepdims=True)
    lo = gtop * per
    el = jnp.where(lane >= lo, jnp.where(lane < lo + per, logits, NEG_BIG), NEG_BIG)
    m1 = jnp.max(el, axis=-1, keepdims=True)
    i1 = jnp.min(jnp.where(el == m1, lane, far), axis=-1, keepdims=True)
    el2 = jnp.where(lane == i1, NEG_BIG, el)
    m2 = jnp.max(el2, axis=-1, keepdims=True)
    i2 = jnp.min(jnp.where(el2 == m2, lane, far), axis=-1, keepdims=True)
    e21 = jnp.exp(m2 - m1)
    w1 = g_w / (1.0 + e21)
    w2 = g_w * e21 / (1.0 + e21)
    ridx_ref[...] = jnp.where(lane == 0.0, i1, jnp.where(lane == 1.0, i2, 0.0)).astype(jnp.int32)
    rw_ref[...] = jnp.where(lane == 0.0, w1, jnp.where(lane == 1.0, w2, 0.0))


def _mix_out(x, ypool, yscan, ussm, ymla, mod, dsk, wglu, gns, gnm, wo, g2, wrh, wrl, br, *, seq, n_exp, n_grp):
    n, d = x.shape
    tm = TOKEN_TILE
    tpb = seq // tm
    row = lambda w: pl.BlockSpec((tm, w), lambda i: (i, 0))
    return pl.pallas_call(
        functools.partial(_mix_out_body, n_exp=n_exp, n_grp=n_grp),
        grid=(n // tm,),
        in_specs=[row(d), row(ypool.shape[1]), row(yscan.shape[1]), row(ussm.shape[1]), row(ymla.shape[1]),
                  pl.BlockSpec((1, 6, d), lambda i: (i // tpb, 0, 0)),
                  _const_spec(dsk.shape), _const_spec(wglu.shape), _const_spec(gns.shape), _const_spec(gnm.shape),
                  _const_spec(wo.shape), _const_spec(g2.shape), _const_spec(wrh.shape), _const_spec(wrl.shape),
                  _const_spec(br.shape)],
        out_specs=[row(d), row(d), row(LANES), row(LANES)],
        out_shape=[jax.ShapeDtypeStruct((n, d), F32), jax.ShapeDtypeStruct((n, d), F32),
                   jax.ShapeDtypeStruct((n, LANES), jnp.int32), jax.ShapeDtypeStruct((n, LANES), F32)],
        compiler_params=_params(("arbitrary",)),
        name="mix_out",
    )(x, ypool, yscan, ussm, ymla, mod, dsk, wglu, gns, gnm, wo, g2, wrh, wrl, br)


def _route_plan(ridx, n_exp, blk):
    expert = ridx[:, :TOP_K].reshape(-1)
    a = expert.shape[0]
    onehot = (expert[:, None] == jnp.arange(n_exp, dtype=jnp.int32)[None, :]).astype(jnp.int32)
    csum = jnp.cumsum(onehot, axis=0)
    counts = csum[-1]
    padded = (counts + blk - 1) // blk * blk
    pad_end = jnp.cumsum(padded)
    pad_start = pad_end - padded
    dest = jnp.sum(onehot * (csum - 1 + pad_start[None, :]), axis=1).astype(jnp.int32)
    nb = a // blk + n_exp
    bstart = jnp.arange(nb, dtype=jnp.int32) * blk
    be = jnp.minimum(jnp.searchsorted(pad_end, bstart, side='right'), n_exp - 1).astype(jnp.int32)
    nv = jnp.clip(counts[be] - (bstart - pad_start[be]), 0, blk).astype(jnp.int32)
    return dest, be, nv, nb


def _dispatch_body(nv_ref, dest_ref, h_ref, xs_hbm, zero_ref, sem, zsem, *, tm, blk, nb):
    def zero_copy(j):
        return pltpu.make_async_copy(zero_ref, xs_hbm.at[pl.ds(j * blk, blk)], zsem)

    @pl.when(pl.program_id(0) == 0)
    def _():
        zero_ref[...] = jnp.zeros(zero_ref.shape, zero_ref.dtype)

        def fill(j, carry):
            @pl.when(nv_ref[j] < blk)
            def _():
                zero_copy(j).start()
            return carry

        def drain(j, carry):
            @pl.when(nv_ref[j] < blk)
            def _():
                zero_copy(j).wait()
            return carry

        lax.fori_loop(0, nb, fill, 0)
        lax.fori_loop(0, nb, drain, 0)

    def issue(r, carry):
        for k in range(TOP_K):
            pltpu.make_async_copy(h_ref.at[pl.ds(r, 1)], xs_hbm.at[pl.ds(dest_ref[TOP_K * r + k], 1)], sem).start()
        return carry

    lax.fori_loop(0, tm, issue, 0)
    for k in range(TOP_K):
        pltpu.make_async_copy(h_ref, xs_hbm.at[pl.ds(0, tm)], sem).wait()


def _dispatch(nv, dest, h2, nb):
    n, d = h2.shape
    tm = TOKEN_TILE
    blk = MOE_ROWS
    grid_spec = pltpu.PrefetchScalarGridSpec(
        num_scalar_prefetch=1,
        grid=(n // tm,),
        in_specs=[pl.BlockSpec((TOP_K * tm,), lambda i, nv: (i,), memory_space=pltpu.SMEM),
                  pl.BlockSpec((tm, d), lambda i, nv: (i, 0))],
        out_specs=pl.BlockSpec(memory_space=pl.ANY),
        scratch_shapes=[pltpu.VMEM((blk, d), h2.dtype), pltpu.SemaphoreType.DMA(()), pltpu.SemaphoreType.DMA(())],
    )
    return pl.pallas_call(
        functools.partial(_dispatch_body, tm=tm, blk=blk, nb=nb),
        grid_spec=grid_spec,
        out_shape=jax.ShapeDtypeStruct((nb * blk, d), h2.dtype),
        compiler_params=pltpu.CompilerParams(dimension_semantics=("arbitrary",), vmem_limit_bytes=VMEM_LIMIT,
                                             has_side_effects=True),
        name="moe_dispatch",
    )(nv, dest, h2)


def _expert_body(be_ref, nv_ref, xs_ref, w1_ref, w3_ref, w2_ref, ys_ref, w1b, w3b, w2b, *, blk):
    j = pl.program_id(0)
    e = be_ref[j]
    prev = be_ref[jnp.maximum(j - 1, 0)]

    @pl.when(jnp.logical_or(j == 0, e != prev))
    def _():
        w1b[...] = w1_ref[0].astype(BF16)
        w3b[...] = w3_ref[0].astype(BF16)
        w2b[...] = w2_ref[0].astype(BF16)

    nv = nv_ref[j]

    @pl.when(nv > 0)
    def _():
        x = xs_ref[...].astype(BF16)
        a = _dot(x, w1b[...])
        b = _dot(x, w3b[...])
        ys_ref[...] = _dot((a * jax.nn.sigmoid(a) * b).astype(BF16), w2b[...])

    @pl.when(nv == 0)
    def _():
        ys_ref[...] = jnp.zeros(ys_ref.shape, F32)


def _experts(be, nv, xs, w1, w3, w2, nb):
    blk = MOE_ROWS
    _, d, de = w1.shape
    grid_spec = pltpu.PrefetchScalarGridSpec(
        num_scalar_prefetch=2,
        grid=(nb,),
        in_specs=[pl.BlockSpec((blk, d), lambda j, be, nv: (j, 0)),
                  pl.BlockSpec((1, d, de), lambda j, be, nv: (be[j], 0, 0)),
                  pl.BlockSpec((1, d, de), lambda j, be, nv: (be[j], 0, 0)),
                  pl.BlockSpec((1, de, d), lambda j, be, nv: (be[j], 0, 0))],
        out_specs=pl.BlockSpec((blk, d), lambda j, be, nv: (j, 0)),
        scratch_shapes=[pltpu.VMEM((d, de), BF16), pltpu.VMEM((d, de), BF16), pltpu.VMEM((de, d), BF16)],
    )
    return pl.pallas_call(
        functools.partial(_expert_body, blk=blk),
        grid_spec=grid_spec,
        out_shape=jax.ShapeDtypeStruct((nb * blk, d), F32),
        compiler_params=_params(("arbitrary",)),
        name="moe_experts",
    )(be, nv, xs, w1, w3, w2)


def _combine_body(dest_ref, x1_ref, rw_ref, mod_ref, fg_ref, ys_hbm, out_ref, buf, sem, *, tm, final):
    def issue(r, carry):
        for k in range(TOP_K):
            pltpu.make_async_copy(ys_hbm.at[pl.ds(dest_ref[TOP_K * r + k], 1)], buf.at[k, pl.ds(r, 1)], sem).start()
        return carry

    lax.fori_loop(0, tm, issue, 0)
    for k in range(TOP_K):
        pltpu.make_async_copy(ys_hbm.at[pl.ds(0, tm)], buf.at[k], sem).wait()
    w = rw_ref[...]
    y = buf[0] * w[:, 0:1] + buf[1] * w[:, 1:2]
    x2 = x1_ref[...] + mod_ref[0][5:6] * y
    out_ref[...] = _rms(x2, fg_ref[...]) if final else x2


def _combine(dest, x1, rw, mod, fg, ys, *, seq, final):
    n, d = x1.shape
    tm = TOKEN_TILE
    tpb = seq // tm
    row = lambda w: pl.BlockSpec((tm, w), lambda i: (i, 0))
    return pl.pallas_call(
        functools.partial(_combine_body, tm=tm, final=final),
        grid=(n // tm,),
        in_specs=[pl.BlockSpec((TOP_K * tm,), lambda i: (i,), memory_space=pltpu.SMEM),
                  row(d), row(LANES),
                  pl.BlockSpec((1, 6, d), lambda i: (i // tpb, 0, 0)),
                  pl.BlockSpec((1, d), lambda i: (0, 0)),
                  pl.BlockSpec(memory_space=pl.ANY)],
        out_specs=row(d),
        out_shape=jax.ShapeDtypeStruct((n, d), F32),
        scratch_shapes=[pltpu.VMEM((TOP_K, tm, d), F32), pltpu.SemaphoreType.DMA(())],
        compiler_params=_params(("arbitrary",)),
        name="moe_combine",
    )(dest, x1, rw, mod, fg, ys)


def _rot_half_cols(w):
    half = w.shape[-1] // 2
    return jnp.concatenate([-w[..., half:], w[..., :half]], axis=-1)


def _layer_weights(l, w_in, pool_w, w_uq, w_ukv, w_out, ssm_w_glu, router_w_group, router_b_group,
                   router_w_expert, router_b_expert):
    d = w_in.shape[1]
    heads = w_uq.shape[2]
    wi = w_in[l]
    kpe = wi[:, -QK_ROPE:]
    pad = jnp.zeros((d, LANES - QK_ROPE), F32)
    win = jnp.concatenate([wi[:, :-QK_ROPE], kpe, pad, _rot_half_cols(kpe), pad], axis=1).astype(BF16)
    uq = w_uq[l]
    r = uq.shape[0]
    zq = jnp.zeros((r, heads, LANES - QK_ROPE), F32)
    wuq = jnp.concatenate([uq, zq], axis=-1).reshape(r, heads * 2 * LANES).astype(BF16)
    wuqr = jnp.concatenate([_rot_half_cols(uq[..., QK_NOPE:]), zq], axis=-1).reshape(r, heads * LANES).astype(BF16)
    ukv = w_ukv[l]
    wukv = jnp.concatenate([ukv[..., :QK_NOPE].reshape(ukv.shape[0], -1),
                            ukv[..., QK_NOPE:].reshape(ukv.shape[0], -1)], axis=1).astype(BF16)
    n_grp = router_w_group.shape[-1]
    n_exp = router_w_expert.shape[-1]
    wr = jnp.concatenate([router_w_expert[l], router_w_group[l],
                          jnp.zeros((d, LANES - n_exp - n_grp), F32)], axis=1)
    wrh, wrl = _split_bf16(wr)
    br = jnp.concatenate([router_b_expert[l], router_b_group[l],
                          jnp.zeros((LANES - n_exp - n_grp,), F32)]).reshape(1, LANES)
    return dict(win=win, poolw=pool_w[l].astype(BF16), wuq=wuq, wuqr=wuqr, wukv=wukv,
                wo=w_out[l].astype(BF16), wglu=ssm_w_glu[l].astype(BF16), wrh=wrh, wrl=wrl, br=br)


def kernel(x, c, positions, w_ada, b_ada, norm1_g, w_in, pool_w, pool_scale, ssm_lam_re, ssm_lam_im, ssm_log_dt, ssm_b_re, ssm_b_im, ssm_c_re, ssm_c_im, ssm_d, ssm_w_glu, q_norm_g, kv_norm_g, w_uq, w_ukv, out_norm_g, w_out, norm2_g, router_w_group, router_b_group, router_w_expert, router_b_expert, w_gate, w_up, w_down, final_g):
    batch, seq, d = x.shape
    depth = w_ada.shape[0]
    n = batch * seq
    heads = w_uq.shape[2]
    n_grp = router_w_group.shape[-1]
    n_exp = router_w_expert.shape[-1]
    pw = pool_w.shape[1] * pool_w.shape[2]
    sw = ssm_d.shape[-1]
    assert SSM_CHUNK * SSM_GROUP == MXU_DIM and pw == len(POOL_WINDOWS) * LANES and sw % LANES == 0
    assert seq % ATTN_TQ == 0 and seq % TOKEN_TILE == 0 and (n * TOP_K) % MOE_ROWS == 0

    mod_all = _ada_mod(c, w_ada, b_ada)
    cs, sn = _rope_tables(positions)
    xf = x.reshape(n, d)
    for l in range(depth):
        wts = _layer_weights(l, w_in, pool_w, w_uq, w_ukv, w_out, ssm_w_glu, router_w_group, router_b_group,
                             router_w_expert, router_b_expert)
        mod = mod_all[l]
        gn = out_norm_g[l]
        ypool, ussm, q, k, v = _mix_in(
            xf, mod, norm1_g[l].reshape(1, d), wts['win'], wts['poolw'], pool_scale[l].reshape(1, pw),
            gn[:pw].reshape(1, pw), q_norm_g[l].reshape(1, -1), kv_norm_g[l].reshape(1, -1),
            wts['wuq'], wts['wuqr'], wts['wukv'], cs, sn, seq=seq, heads=heads)
        m, bc, cc, tab = _ssm_tables(ssm_lam_re[l], ssm_lam_im[l], ssm_log_dt[l], ssm_b_re[l], ssm_b_im[l],
                                     ssm_c_re[l], ssm_c_im[l])
        yscan = _s5_scan(ussm, m, bc, cc, tab, batch=batch)
        ymla = _mla_attn(q, k, v, batch=batch, seq=seq, heads=heads)
        x1, h2, ridx, rw = _mix_out(
            xf, ypool, yscan, ussm, ymla, mod, ssm_d[l].reshape(1, sw), wts['wglu'],
            gn[pw:pw + sw].reshape(1, sw), gn[pw + sw:].reshape(1, -1), wts['wo'], norm2_g[l].reshape(1, d),
            wts['wrh'], wts['wrl'], wts['br'], seq=seq, n_exp=n_exp, n_grp=n_grp)
        dest, be, nv, nb = _route_plan(ridx, n_exp, MOE_ROWS)
        xs = _dispatch(nv, dest, h2, nb)
        ys = _experts(be, nv, xs, w_gate[l], w_up[l], w_down[l], nb)
        xf = _combine(dest, x1, rw, mod, final_g.reshape(1, d), ys, seq=seq, final=(l == depth - 1))
    return xf.reshape(batch, seq, d)
```

```python
import functools
import math

import jax
import jax.numpy as jnp
from jax import lax
from jax.experimental import pallas as pl
from jax.experimental.pallas import tpu as pltpu

F32 = jnp.float32
BF16 = jnp.bfloat16

POOL_WINDOWS = (2, 4, 8, 16)
SSM_GROUP = 16
QK_NOPE = 128
QK_ROPE = 64
V_HEAD = 128
ROPE_THETA = 10000.0
EPS = 1e-6
TOP_K = 2

LANES = 128
SUBLANES = 8
MXU_DIM = 256
VMEM_LIMIT = 56 * 1024 * 1024

SSM_CHUNK = 16
S5_ROWS = 4096
TOKEN_TILE = 256
ATTN_TB = 512
ATTN_SUB = 2
MOE_ROWS = 256
NEG_BIG = -1e30


def _dot(a, b):
    return jnp.dot(a, b, preferred_element_type=F32)


def _split_bf16(a):
    hi = a.astype(BF16)
    lo = (a - hi.astype(F32)).astype(BF16)
    return hi, lo


def _rms(x, g):
    return x * lax.rsqrt(jnp.mean(x * x, axis=-1, keepdims=True) + EPS) * g


def _params(semantics):
    return pltpu.CompilerParams(dimension_semantics=semantics, vmem_limit_bytes=VMEM_LIMIT)


def _const_spec(shape):
    nd = len(shape)
    return pl.BlockSpec(shape, lambda *_: (0,) * nd, pipeline_mode=pl.Buffered(1))


def _ada_body(c_ref, w_ref, b_ref, o_ref):
    ch, cl = _split_bf16(c_ref[...])
    wh, wl = _split_bf16(w_ref[0])
    o_ref[0] = _dot(ch, wh) + _dot(cl, wh) + _dot(ch, wl) + b_ref[0]


def _ada_mod(c, w_ada, b_ada):
    depth, d, n6 = w_ada.shape
    b = c.shape[0]
    tn = 1024
    c_pad = jnp.zeros((SUBLANES, d), F32).at[:b].set(c)
    out = pl.pallas_call(
        _ada_body,
        grid=(depth, n6 // tn),
        in_specs=[pl.BlockSpec((SUBLANES, d), lambda l, j: (0, 0)),
                  pl.BlockSpec((1, d, tn), lambda l, j: (l, 0, j)),
                  pl.BlockSpec((1, 1, tn), lambda l, j: (l, 0, j))],
        out_specs=pl.BlockSpec((1, SUBLANES, tn), lambda l, j: (l, 0, j)),
        out_shape=jax.ShapeDtypeStruct((depth, SUBLANES, n6), F32),
        compiler_params=_params(("arbitrary", "arbitrary")),
        name="ada_mod",
    )(c_pad, w_ada, b_ada.reshape(depth, 1, n6))
    return out[:, :b].reshape(depth, b, 6, d)


def _rope_body(pos_ref, invf_ref, cos_ref, sin_ref):
    ang = pos_ref[...] * invf_ref[...]
    cos_ref[...] = jnp.cos(ang)
    sin_ref[...] = jnp.sin(ang)


def _rope_tables(positions):
    n = positions.size
    half = QK_ROPE // 2
    per_row = LANES // half
    inv_freq = jnp.power(ROPE_THETA, -jnp.arange(0, QK_ROPE, 2, dtype=F32) / QK_ROPE)
    pos = jnp.repeat(positions.reshape(n // per_row, per_row).astype(F32), half, axis=1)
    invf = jnp.tile(inv_freq, per_row).reshape(1, LANES)
    rows = n // per_row
    tr = min(rows, 1024)
    cos, sin = pl.pallas_call(
        _rope_body,
        grid=(rows // tr,),
        in_specs=[pl.BlockSpec((tr, LANES), lambda i: (i, 0)), pl.BlockSpec((1, LANES), lambda i: (0, 0))],
        out_specs=[pl.BlockSpec((tr, LANES), lambda i: (i, 0))] * 2,
        out_shape=[jax.ShapeDtypeStruct((rows, LANES), F32)] * 2,
        compiler_params=_params(("arbitrary",)),
        name="rope_tables",
    )(pos, invf)
    z = jnp.zeros((n, LANES - QK_ROPE), F32)
    cos, sin = cos.reshape(n, half), sin.reshape(n, half)
    return jnp.concatenate([cos, cos, z], axis=1), jnp.concatenate([sin, sin, z], axis=1)


def _mix_in_body(x_ref, mod_ref, g1_ref, win_ref, poolw_ref, pscale_ref, gnp_ref, qg_ref, kvg_ref,
                 wuq_ref, wuqr_ref, wukv_ref, cs_ref, sn_ref,
                 ypool_ref, ussm_ref, q_ref, k_ref, v_ref, ext_ref, *, tiles_per_batch, tm, heads, scale):
    tin = pl.program_id(0) % tiles_per_batch
    mod = mod_ref[0]
    sh_a, sc_a = mod[0:1], mod[1:2]
    h = _rms(x_ref[...], g1_ref[...]) * (1.0 + sc_a) + sh_a
    z = _dot(h.astype(BF16), win_ref[...])

    halo = max(POOL_WINDOWS)
    pw = len(POOL_WINDOWS) * LANES

    @pl.when(tin == 0)
    def _():
        ext_ref[0:halo, :] = jnp.zeros((halo, pw), F32)

    zp = z[:, 0:pw]
    ext_ref[halo:halo + tm, :] = zp
    t = tin * tm + lax.broadcasted_iota(jnp.int32, (tm, 1), 0)
    ys = []
    for gi, w in enumerate(POOL_WINDOWS):
        cols = slice(gi * LANES, (gi + 1) * LANES)
        tok = zp[:, cols]
        s = tok
        for j in range(1, w):
            s = s + ext_ref[halo - j:halo - j + tm, cols]
        cnt = jnp.minimum(t + 1, w).astype(F32)
        ys.append(_dot((s / cnt - tok).astype(BF16), poolw_ref[gi]))
    ypool = jnp.concatenate(ys, axis=1) * pscale_ref[...]
    ypool_ref[...] = _rms(ypool, gnp_ref[...]).astype(BF16)
    ext_ref[0:halo, :] = ext_ref[tm:tm + halo, :]

    ussm_ref[...] = z[:, pw:2 * pw]

    cs, sn = cs_ref[...], sn_ref[...]
    qn = _rms(z[:, 1024:1536], qg_ref[...]).astype(BF16)
    qm = _dot(qn, wuq_ref[...])
    qr = _dot(qn, wuqr_ref[...])
    cs_q, sn_q = cs * scale, sn * scale
    for hh in range(heads):
        o = hh * 2 * LANES
        q_ref[hh, :, 0:LANES] = (qm[:, o:o + LANES] * scale).astype(BF16)
        q_ref[hh, :, LANES:2 * LANES] = (qm[:, o + LANES:o + 2 * LANES] * cs_q
                                         + qr[:, hh * LANES:(hh + 1) * LANES] * sn_q).astype(BF16)
    kvn = _rms(z[:, 1536:1792], kvg_ref[...]).astype(BF16)
    kv = _dot(kvn, wukv_ref[...])
    kpe = (z[:, 1792:1920] * cs + z[:, 1920:2048] * sn).astype(BF16)
    for hh in range(heads):
        k_ref[hh, :, 0:LANES] = kv[:, hh * LANES:(hh + 1) * LANES].astype(BF16)
        k_ref[hh, :, LANES:2 * LANES] = kpe
    v_ref[...] = kv[:, heads * LANES:].astype(BF16)


def _mix_in(x, mod, g1, win, poolw, pscale, gnp, qg, kvg, wuq, wuqr, wukv, cs, sn, *, seq, heads):
    n, d = x.shape
    tm = TOKEN_TILE
    tpb = seq // tm
    pw = len(POOL_WINDOWS) * LANES
    halo = max(POOL_WINDOWS)
    scale = float((QK_NOPE + QK_ROPE) ** -0.5 * math.log2(math.e))
    row = lambda w: pl.BlockSpec((tm, w), lambda i: (i, 0))
    hd = pl.BlockSpec((heads, tm, 2 * LANES), lambda i: (0, i, 0))
    return pl.pallas_call(
        functools.partial(_mix_in_body, tiles_per_batch=tpb, tm=tm, heads=heads, scale=scale),
        grid=(n // tm,),
        in_specs=[row(d),
                  pl.BlockSpec((1, 6, d), lambda i: (i // tpb, 0, 0)),
                  _const_spec(g1.shape), _const_spec(win.shape), _const_spec(poolw.shape),
                  _const_spec(pscale.shape), _const_spec(gnp.shape), _const_spec(qg.shape),
                  _const_spec(kvg.shape), _const_spec(wuq.shape), _const_spec(wuqr.shape),
                  _const_spec(wukv.shape), row(LANES), row(LANES)],
        out_specs=[row(pw), row(pw), hd, hd, row(heads * V_HEAD)],
        out_shape=[jax.ShapeDtypeStruct((n, pw), BF16), jax.ShapeDtypeStruct((n, pw), F32),
                   jax.ShapeDtypeStruct((heads, n, 2 * LANES), BF16),
                   jax.ShapeDtypeStruct((heads, n, 2 * LANES), BF16),
                   jax.ShapeDtypeStruct((n, heads * V_HEAD), BF16)],
        scratch_shapes=[pltpu.VMEM((tm + halo, pw), F32)],
        compiler_params=_params(("arbitrary",)),
        name="mix_in",
    )(x, mod, g1, win, poolw, pscale, gnp, qg, kvg, wuq, wuqr, wukv, cs, sn)


def _s5_body(u_ref, m_ref, bc_ref, cc_ref, tab_ref, y_ref, xcat_ref, sre_ref, sim_ref, car_ref, *, nc):
    t_ch = SSM_CHUNK
    sw = tab_ref.shape[-1]

    @pl.when(pl.program_id(2) == 0)
    def _():
        car_ref[...] = jnp.zeros(car_ref.shape, F32)

    for t in range(t_ch):
        xcat_ref[:, t * LANES:(t + 1) * LANES] = u_ref[pl.ds(t, nc, stride=t_ch), :].astype(BF16)
    xcat = xcat_ref[...]
    y_in = _dot(xcat, m_ref[0])
    x = _dot(xcat, bc_ref[0])
    sre_ref[...] = x[:, 0:sw]
    sim_ref[...] = x[:, sw:2 * sw]
    first_row = lax.broadcasted_iota(jnp.int32, (SUBLANES, sw), 0) == 0

    def block(b, carry):
        cre, cim = carry
        r0 = pl.multiple_of(b * SUBLANES, SUBLANES)
        re = sre_ref[pl.ds(r0, SUBLANES), :]
        im = sim_ref[pl.ds(r0, SUBLANES), :]
        for n, k in enumerate((1, 2, 4)):
            tr, ti = tab_ref[0, 2 * n], tab_ref[0, 2 * n + 1]
            pre, pim = pltpu.roll(re, k, 0), pltpu.roll(im, k, 0)
            re, im = re + tr * pre - ti * pim, im + tr * pim + ti * pre
        pr, pi = tab_ref[0, 6], tab_ref[0, 7]
        ore = re + pr * cre - pi * cim
        oim = im + pr * cim + pi * cre
        sre_ref[pl.ds(r0, SUBLANES), :] = jnp.where(first_row, cre, pltpu.roll(ore, 1, 0))
        sim_ref[pl.ds(r0, SUBLANES), :] = jnp.where(first_row, cim, pltpu.roll(oim, 1, 0))
        last = SUBLANES - 1
        return (jnp.broadcast_to(ore[last:last + 1, :], (SUBLANES, sw)),
                jnp.broadcast_to(oim[last:last + 1, :], (SUBLANES, sw)))

    cre, cim = lax.fori_loop(0, nc // SUBLANES, block, (car_ref[0], car_ref[1]))
    car_ref[0] = cre
    car_ref[1] = cim
    sp = jnp.concatenate([sre_ref[...], sim_ref[...]], axis=1).astype(BF16)
    y = y_in + _dot(sp, cc_ref[0])
    for t in range(t_ch):
        y_ref[pl.ds(t, nc, stride=t_ch), :] = y[:, t * LANES:(t + 1) * LANES]


def _s5_scan(u, m, bc, cc, tab, *, batch):
    n, width = u.shape
    tiles = width // LANES
    seq = n // batch
    rs = min(S5_ROWS, seq)
    steps = seq // rs
    nc = rs // SSM_CHUNK
    sw = tab.shape[-1]
    kw = SSM_CHUNK * LANES
    blk = pl.BlockSpec((rs, LANES), lambda j, b, r: (b * steps + r, j))
    return pl.pallas_call(
        functools.partial(_s5_body, nc=nc),
        grid=(tiles, batch, steps),
        in_specs=[blk,
                  pl.BlockSpec((1, kw, kw), lambda j, b, r: (j, 0, 0)),
                  pl.BlockSpec((1, kw, 2 * sw), lambda j, b, r: (j, 0, 0)),
                  pl.BlockSpec((1, 2 * sw, kw), lambda j, b, r: (j, 0, 0)),
                  pl.BlockSpec((1, 8, SUBLANES, sw), lambda j, b, r: (j, 0, 0, 0))],
        out_specs=blk,
        out_shape=jax.ShapeDtypeStruct((n, width), F32),
        scratch_shapes=[pltpu.VMEM((nc, kw), BF16), pltpu.VMEM((nc, sw), F32), pltpu.VMEM((nc, sw), F32),
                        pltpu.VMEM((2, SUBLANES, sw), F32)],
        compiler_params=_params(("arbitrary", "arbitrary", "arbitrary")),
        name="s5_scan",
    )(u, m, bc, cc, tab)


def _ssm_tables(lam_re, lam_im, log_dt, b_re, b_im, c_re, c_im):
    hp = lax.Precision.HIGHEST
    g, p = lam_re.shape
    hh = b_re.shape[-1]
    t = SSM_CHUNK
    gq = LANES // hh
    tiles = g // gq
    dt = jnp.exp(log_dt)[:, None]
    mag = jnp.exp(lam_re * dt)
    ar, ai = mag * jnp.cos(lam_im * dt), mag * jnp.sin(lam_im * dt)
    den = lam_re * lam_re + lam_im * lam_im
    fr = ((ar - 1.0) * lam_re + ai * lam_im) / den
    fi = (ai * lam_re - (ar - 1.0) * lam_im) / den
    bbr = fr[..., None] * b_re - fi[..., None] * b_im
    bbi = fr[..., None] * b_im + fi[..., None] * b_re

    def powers(xr, xi, count):
        prs, pis = [jnp.ones_like(xr)], [jnp.zeros_like(xi)]
        for _ in range(count):
            prs.append(prs[-1] * xr - pis[-1] * xi)
            pis.append(prs[-2] * xi + pis[-1] * xr)
        return jnp.stack(prs), jnp.stack(pis)

    pr, pi = powers(ar, ai, t)
    car = c_re[None] * pr[:, :, None, :] - c_im[None] * pi[:, :, None, :]
    cai = c_im[None] * pr[:, :, None, :] + c_re[None] * pi[:, :, None, :]
    kern = (jnp.einsum('kgop,gpi->gkio', car[:t], bbr, precision=hp)
            - jnp.einsum('kgop,gpi->gkio', cai[:t], bbi, precision=hp))
    lag = jnp.arange(t)[None, :] - jnp.arange(t)[:, None]
    toep = jnp.where((lag >= 0)[None, :, :, None, None], kern[:, jnp.clip(lag, 0)], 0.0)
    eye = jnp.eye(gq, dtype=F32)
    kw = t * gq * hh
    toep = toep.reshape(tiles, gq, t, t, hh, hh)
    m = (toep.transpose(0, 2, 1, 4, 3, 5)[:, :, :, :, :, None, :]
         * eye[None, None, :, None, None, :, None]).reshape(tiles, kw, kw)
    rev = jnp.arange(t - 1, -1, -1)
    bcr = (pr[rev][..., None] * bbr[None] - pi[rev][..., None] * bbi[None]).transpose(1, 0, 3, 2)
    bci = (pr[rev][..., None] * bbi[None] + pi[rev][..., None] * bbr[None]).transpose(1, 0, 3, 2)
    bri = jnp.stack([bcr, bci], axis=3).reshape(tiles, gq, t, hh, 2, p)
    bc = (bri.transpose(0, 2, 1, 3, 4, 5)[:, :, :, :, :, None, :]
          * eye[None, None, :, None, None, :, None]).reshape(tiles, kw, 2 * gq * p)
    ccr = car[1:].transpose(1, 3, 0, 2)
    cci = -cai[1:].transpose(1, 3, 0, 2)
    cri = jnp.stack([ccr, cci], axis=1).reshape(tiles, gq, 2, p, t, hh)
    cc = (cri.transpose(0, 2, 1, 3, 4, 5)[:, :, :, :, :, None, :]
          * eye[None, None, :, None, None, :, None]).reshape(tiles, 2 * gq * p, kw)
    a16r, a16i = pr[t].reshape(tiles, gq * p), pi[t].reshape(tiles, gq * p)
    qr, qi = powers(a16r, a16i, SUBLANES)
    sub = jnp.arange(SUBLANES)
    tabs = []
    for k in (1, 2, 4):
        keep = (sub >= k).astype(F32)[None, :, None]
        tabs += [qr[k][:, None, :] * keep, qi[k][:, None, :] * keep]
    tabs += [qr[1:].transpose(1, 0, 2), qi[1:].transpose(1, 0, 2)]
    tab = jnp.stack(tabs, axis=1)
    return m.astype(BF16), bc.astype(BF16), cc.astype(BF16), tab


def _attn_body(q_ref, k_ref, v_ref, o_ref, *, tb, nsub):
    i = pl.program_id(1)
    qs = [q_ref[0, s * tb:(s + 1) * tb, :] for s in range(nsub)]
    diag = lax.broadcasted_iota(jnp.int32, (tb, tb), 1) <= lax.broadcasted_iota(jnp.int32, (tb, tb), 0)

    def load(j):
        k0 = pl.multiple_of(j * tb, tb)
        return k_ref[0, pl.ds(k0, tb), :], v_ref[pl.ds(k0, tb), :]

    def step(q, kb, vb, carry, masked):
        m, l, acc = carry
        s = lax.dot_general(q, kb, (((1,), (1,)), ((), ())), preferred_element_type=F32)
        if masked:
            s = jnp.where(diag, s, NEG_BIG)
        m_new = jnp.maximum(m, jnp.max(s, axis=-1, keepdims=True))
        p = jnp.exp2(s - m_new)
        alpha = jnp.exp2(m - m_new)
        l = alpha * l + jnp.sum(p, axis=-1, keepdims=True)
        acc = alpha * acc + _dot(p.astype(BF16), vb)
        return m_new, l, acc

    def body(j, carries):
        kb, vb = load(j)
        return tuple(step(qs[s], kb, vb, carries[s], False) for s in range(nsub))

    init = (jnp.full((tb, 1), NEG_BIG, F32), jnp.zeros((tb, 1), F32), jnp.zeros((tb, V_HEAD), F32))
    n_full = i * nsub
    carries = list(lax.fori_loop(0, n_full, body, (init,) * nsub))
    for d in range(nsub):
        kb, vb = load(n_full + d)
        for s in range(d, nsub):
            carries[s] = step(qs[s], kb, vb, carries[s], s == d)
    for s in range(nsub):
        _, l, acc = carries[s]
        o_ref[s * tb:(s + 1) * tb, :] = (acc / l).astype(BF16)


def _mla_attn(q, k, v, *, batch, seq, heads):
    tb, nsub = ATTN_TB, ATTN_SUB
    tq = tb * nsub
    nq = seq // tq
    n = batch * seq
    return pl.pallas_call(
        functools.partial(_attn_body, tb=tb, nsub=nsub),
        grid=(batch * heads, nq),
        in_specs=[pl.BlockSpec((1, tq, 2 * LANES), lambda bh, i: (bh % heads, (bh // heads) * nq + i, 0)),
                  pl.BlockSpec((1, seq, 2 * LANES), lambda bh, i: (bh % heads, bh // heads, 0)),
                  pl.BlockSpec((seq, V_HEAD), lambda bh, i: (bh // heads, bh % heads))],
        out_specs=pl.BlockSpec((tq, V_HEAD), lambda bh, i: ((bh // heads) * nq + i, bh % heads)),
        out_shape=jax.ShapeDtypeStruct((n, heads * V_HEAD), BF16),
        compiler_params=_params(("arbitrary", "arbitrary")),
        name="mla_attn",
    )(q, k, v)


def _gelu_tanh(x):
    return 0.5 * x * (1.0 + jnp.tanh(math.sqrt(2.0 / math.pi) * (x + 0.044715 * (x * x * x))))


def _mix_out_body(x_ref, yp_ref, ysc_ref, us_ref, ym_ref, mod_ref, dsk_ref, wglu_ref, gns_ref, gnm_ref,
                  wo_ref, g2_ref, wrh_ref, wrl_ref, br_ref,
                  x1_ref, h2_ref, ridx_ref, rw_ref, *, n_exp, n_grp):
    mod = mod_ref[0]
    g_a, sh_f, sc_f = mod[2:3], mod[3:4], mod[4:5]
    ys = _gelu_tanh(ysc_ref[...] + dsk_ref[...] * us_ref[...])
    ys = ys * jax.nn.sigmoid(_dot(ys.astype(BF16), wglu_ref[...]))
    ysn = _rms(ys, gns_ref[...]).astype(BF16)
    ymn = _rms(ym_ref[...].astype(F32), gnm_ref[...]).astype(BF16)
    wp, ws = yp_ref.shape[1], ysc_ref.shape[1]
    o = (_dot(yp_ref[...], wo_ref[0:wp, :]) + _dot(ysn, wo_ref[wp:wp + ws, :])
         + _dot(ymn, wo_ref[wp + ws:, :]))
    x1 = x_ref[...] + g_a * o
    x1_ref[...] = x1
    h2 = _rms(x1, g2_ref[...]) * (1.0 + sc_f) + sh_f
    h2_ref[...] = h2

    hh, hl = _split_bf16(h2)
    logits = _dot(hh, wrh_ref[...]) + _dot(hl, wrh_ref[...]) + _dot(hh, wrl_ref[...]) + br_ref[...]
    lane = lax.broadcasted_iota(jnp.int32, logits.shape, 1).astype(F32)
    far = float(2 * LANES)
    per = n_exp // n_grp
    gl = jnp.where(lane >= n_exp, jnp.where(lane < n_exp + n_grp, logits, NEG_BIG), NEG_BIG)
    gmax = jnp.max(gl, axis=-1, keepdims=True)
    gtop = jnp.min(jnp.where(gl == gmax, lane, far), axis=-1, keepdims=True) - n_exp
    g_w = 1.0 / jnp.sum(jnp.exp(gl - gmax), axis=-1, keepdims=True)
    lo = gtop * per
    el = jnp.where(lane >= lo, jnp.where(lane < lo + per, logits, NEG_BIG), NEG_BIG)
    m1 = jnp.max(el, axis=-1, keepdims=True)
    i1 = jnp.min(jnp.where(el == m1, lane, far), axis=-1, keepdims=True)
    el2 = jnp.where(lane == i1, NEG_BIG, el)
    m2 = jnp.max(el2, axis=-1, keepdims=True)
    i2 = jnp.min(jnp.where(el2 == m2, lane, far), axis=-1, keepdims=True)
    e21 = jnp.exp(m2 - m1)
    w1 = g_w / (1.0 + e21)
    w2 = g_w * e21 / (1.0 + e21)
    ridx_ref[...] = jnp.where(lane == 0.0, i1, jnp.where(lane == 1.0, i2, 0.0)).astype(jnp.int32)
    rw_ref[...] = jnp.where(lane == 0.0, w1, jnp.where(lane == 1.0, w2, 0.0))


def _mix_out(x, ypool, yscan, ussm, ymla, mod, dsk, wglu, gns, gnm, wo, g2, wrh, wrl, br, *, seq, n_exp, n_grp):
    n, d = x.shape
    tm = TOKEN_TILE
    tpb = seq // tm
    row = lambda w: pl.BlockSpec((tm, w), lambda i: (i, 0))
    return pl.pallas_call(
        functools.partial(_mix_out_body, n_exp=n_exp, n_grp=n_grp),
        grid=(n // tm,),
        in_specs=[row(d), row(ypool.shape[1]), row(yscan.shape[1]), row(ussm.shape[1]), row(ymla.shape[1]),
                  pl.BlockSpec((1, 6, d), lambda i: (i // tpb, 0, 0)),
                  _const_spec(dsk.shape), _const_spec(wglu.shape), _const_spec(gns.shape), _const_spec(gnm.shape),
                  _const_spec(wo.shape), _const_spec(g2.shape), _const_spec(wrh.shape), _const_spec(wrl.shape),
                  _const_spec(br.shape)],
        out_specs=[row(d), row(d), row(LANES), row(LANES)],
        out_shape=[jax.ShapeDtypeStruct((n, d), F32), jax.ShapeDtypeStruct((n, d), F32),
                   jax.ShapeDtypeStruct((n, LANES), jnp.int32), jax.ShapeDtypeStruct((n, LANES), F32)],
        compiler_params=_params(("arbitrary",)),
        name="mix_out",
    )(x, ypool, yscan, ussm, ymla, mod, dsk, wglu, gns, gnm, wo, g2, wrh, wrl, br)


def _route_plan(ridx, n_exp, blk):
    expert = ridx[:, :TOP_K].reshape(-1)
    a = expert.shape[0]
    onehot = (expert[:, None] == jnp.arange(n_exp, dtype=jnp.int32)[None, :]).astype(jnp.int32)
    csum = jnp.cumsum(onehot, axis=0)
    counts = csum[-1]
    padded = (counts + blk - 1) // blk * blk
    pad_end = jnp.cumsum(padded)
    pad_start = pad_end - padded
    dest = jnp.sum(onehot * (csum - 1 + pad_start[None, :]), axis=1).astype(jnp.int32)
    nb = a // blk + n_exp
    bstart = jnp.arange(nb, dtype=jnp.int32) * blk
    be = jnp.minimum(jnp.searchsorted(pad_end, bstart, side='right'), n_exp - 1).astype(jnp.int32)
    nv = jnp.clip(counts[be] - (bstart - pad_start[be]), 0, blk).astype(jnp.int32)
    return dest, be, nv, nb


def _dispatch_body(nv_ref, dest_ref, h_ref, xs_hbm, zero_ref, sem, zsem, *, tm, blk, nb):
    def zero_copy(j):
        return pltpu.make_async_copy(zero_ref, xs_hbm.at[pl.ds(j * blk, blk)], zsem)

    @pl.when(pl.program_id(0) == 0)
    def _():
        zero_ref[...] = jnp.zeros(zero_ref.shape, zero_ref.dtype)

        def fill(j, carry):
            @pl.when(nv_ref[j] < blk)
            def _():
                zero_copy(j).start()
            return carry

        def drain(j, carry):
            @pl.when(nv_ref[j] < blk)
            def _():
                zero_copy(j).wait()
            return carry

        lax.fori_loop(0, nb, fill, 0)
        lax.fori_loop(0, nb, drain, 0)

    def issue(r, carry):
        for k in range(TOP_K):
            pltpu.make_async_copy(h_ref.at[pl.ds(r, 1)], xs_hbm.at[pl.ds(dest_ref[TOP_K * r + k], 1)], sem).start()
        return carry

    lax.fori_loop(0, tm, issue, 0)
    for k in range(TOP_K):
        pltpu.make_async_copy(h_ref, xs_hbm.at[pl.ds(0, tm)], sem).wait()


def _dispatch(nv, dest, h2, nb):
    n, d = h2.shape
    tm = TOKEN_TILE
    blk = MOE_ROWS
    grid_spec = pltpu.PrefetchScalarGridSpec(
        num_scalar_prefetch=1,
        grid=(n // tm,),
        in_specs=[pl.BlockSpec((TOP_K * tm,), lambda i, nv: (i,), memory_space=pltpu.SMEM),
                  pl.BlockSpec((tm, d), lambda i, nv: (i, 0))],
        out_specs=pl.BlockSpec(memory_space=pl.ANY),
        scratch_shapes=[pltpu.VMEM((blk, d), h2.dtype), pltpu.SemaphoreType.DMA(()), pltpu.SemaphoreType.DMA(())],
    )
    return pl.pallas_call(
        functools.partial(_dispatch_body, tm=tm, blk=blk, nb=nb),
        grid_spec=grid_spec,
        out_shape=jax.ShapeDtypeStruct((nb * blk, d), h2.dtype),
        compiler_params=pltpu.CompilerParams(dimension_semantics=("arbitrary",), vmem_limit_bytes=VMEM_LIMIT,
                                             has_side_effects=True),
        name="moe_dispatch",
    )(nv, dest, h2)


def _expert_body(be_ref, nv_ref, xs_ref, w1_ref, w3_ref, w2_ref, ys_ref, w1b, w3b, w2b, *, blk):
    j = pl.program_id(0)
    e = be_ref[j]
    prev = be_ref[jnp.maximum(j - 1, 0)]

    @pl.when(jnp.logical_or(j == 0, e != prev))
    def _():
        w1b[...] = w1_ref[0].astype(BF16)
        w3b[...] = w3_ref[0].astype(BF16)
        w2b[...] = w2_ref[0].astype(BF16)

    nv = nv_ref[j]

    @pl.when(nv > 0)
    def _():
        x = xs_ref[...].astype(BF16)
        a = _dot(x, w1b[...])
        b = _dot(x, w3b[...])
        ys_ref[...] = _dot((a * jax.nn.sigmoid(a) * b).astype(BF16), w2b[...])

    @pl.when(nv == 0)
    def _():
        ys_ref[...] = jnp.zeros(ys_ref.shape, F32)


def _experts(be, nv, xs, w1, w3, w2, nb):
    blk = MOE_ROWS
    _, d, de = w1.shape
    grid_spec = pltpu.PrefetchScalarGridSpec(
        num_scalar_prefetch=2,
        grid=(nb,),
        in_specs=[pl.BlockSpec((blk, d), lambda j, be, nv: (j, 0)),
                  pl.BlockSpec((1, d, de), lambda j, be, nv: (be[j], 0, 0)),
                  pl.BlockSpec((1, d, de), lambda j, be, nv: (be[j], 0, 0)),
                  pl.BlockSpec((1, de, d), lambda j, be, nv: (be[j], 0, 0))],
        out_specs=pl.BlockSpec((blk, d), lambda j, be, nv: (j, 0)),
        scratch_shapes=[pltpu.VMEM((d, de), BF16), pltpu.VMEM((d, de), BF16), pltpu.VMEM((de, d), BF16)],
    )
    return pl.pallas_call(
        functools.partial(_expert_body, blk=blk),
        grid_spec=grid_spec,
        out_shape=jax.ShapeDtypeStruct((nb * blk, d), F32),
        compiler_params=_params(("arbitrary",)),
        name="moe_experts",
    )(be, nv, xs, w1, w3, w2)


def _combine_body(dest_ref, x1_ref, rw_ref, mod_ref, fg_ref, ys_hbm, out_ref, buf, sem, *, tm, final):
    def issue(r, carry):
        for k in range(TOP_K):
            pltpu.make_async_copy(ys_hbm.at[pl.ds(dest_ref[TOP_K * r + k], 1)], buf.at[k, pl.ds(r, 1)], sem).start()
        return carry

    lax.fori_loop(0, tm, issue, 0)
    for k in range(TOP_K):
        pltpu.make_async_copy(ys_hbm.at[pl.ds(0, tm)], buf.at[k], sem).wait()
    w = rw_ref[...]
    y = buf[0] * w[:, 0:1] + buf[1] * w[:, 1:2]
    x2 = x1_ref[...] + mod_ref[0][5:6] * y
    out_ref[...] = _rms(x2, fg_ref[...]) if final else x2


def _combine(dest, x1, rw, mod, fg, ys, *, seq, final):
    n, d = x1.shape
    tm = TOKEN_TILE
    tpb = seq // tm
    row = lambda w: pl.BlockSpec((tm, w), lambda i: (i, 0))
    return pl.pallas_call(
        functools.partial(_combine_body, tm=tm, final=final),
        grid=(n // tm,),
        in_specs=[pl.BlockSpec((TOP_K * tm,), lambda i: (i,), memory_space=pltpu.SMEM),
                  row(d), row(LANES),
                  pl.BlockSpec((1, 6, d), lambda i: (i // tpb, 0, 0)),
                  pl.BlockSpec((1, d), lambda i: (0, 0)),
                  pl.BlockSpec(memory_space=pl.ANY)],
        out_specs=row(d),
        out_shape=jax.ShapeDtypeStruct((n, d), F32),
        scratch_shapes=[pltpu.VMEM((TOP_K, tm, d), F32), pltpu.SemaphoreType.DMA(())],
        compiler_params=_params(("arbitrary",)),
        name="moe_combine",
    )(dest, x1, rw, mod, fg, ys)


def _rot_half_cols(w):
    half = w.shape[-1] // 2
    return jnp.concatenate([-w[..., half:], w[..., :half]], axis=-1)


def _layer_weights(l, w_in, pool_w, w_uq, w_ukv, w_out, ssm_w_glu, router_w_group, router_b_group,
                   router_w_expert, router_b_expert):
    d = w_in.shape[1]
    heads = w_uq.shape[2]
    wi = w_in[l]
    kpe = wi[:, -QK_ROPE:]
    pad = jnp.zeros((d, LANES - QK_ROPE), F32)
    win = jnp.concatenate([wi[:, :-QK_ROPE], kpe, pad, _rot_half_cols(kpe), pad], axis=1).astype(BF16)
    uq = w_uq[l]
    r = uq.shape[0]
    zq = jnp.zeros((r, heads, LANES - QK_ROPE), F32)
    wuq = jnp.concatenate([uq, zq], axis=-1).reshape(r, heads * 2 * LANES).astype(BF16)
    wuqr = jnp.concatenate([_rot_half_cols(uq[..., QK_NOPE:]), zq], axis=-1).reshape(r, heads * LANES).astype(BF16)
    ukv = w_ukv[l]
    wukv = jnp.concatenate([ukv[..., :QK_NOPE].reshape(ukv.shape[0], -1),
                            ukv[..., QK_NOPE:].reshape(ukv.shape[0], -1)], axis=1).astype(BF16)
    n_grp = router_w_group.shape[-1]
    n_exp = router_w_expert.shape[-1]
    wr = jnp.concatenate([router_w_expert[l], router_w_group[l],
                          jnp.zeros((d, LANES - n_exp - n_grp), F32)], axis=1)
    wrh, wrl = _split_bf16(wr)
    br = jnp.concatenate([router_b_expert[l], router_b_group[l],
                          jnp.zeros((LANES - n_exp - n_grp,), F32)]).reshape(1, LANES)
    return dict(win=win, poolw=pool_w[l].astype(BF16), wuq=wuq, wuqr=wuqr, wukv=wukv,
                wo=w_out[l].astype(BF16), wglu=ssm_w_glu[l].astype(BF16), wrh=wrh, wrl=wrl, br=br)


def kernel(x, c, positions, w_ada, b_ada, norm1_g, w_in, pool_w, pool_scale, ssm_lam_re, ssm_lam_im, ssm_log_dt, ssm_b_re, ssm_b_im, ssm_c_re, ssm_c_im, ssm_d, ssm_w_glu, q_norm_g, kv_norm_g, w_uq, w_ukv, out_norm_g, w_out, norm2_g, router_w_group, router_b_group, router_w_expert, router_b_expert, w_gate, w_up, w_down, final_g):
    batch, seq, d = x.shape
    depth = w_ada.shape[0]
    n = batch * seq
    heads = w_uq.shape[2]
    n_grp = router_w_group.shape[-1]
    n_exp = router_w_expert.shape[-1]
    pw = pool_w.shape[1] * pool_w.shape[2]
    sw = ssm_d.shape[-1]
    assert SSM_CHUNK * SSM_GROUP == MXU_DIM and pw == len(POOL_WINDOWS) * LANES and sw % LANES == 0
    assert seq % (ATTN_TB * ATTN_SUB) == 0 and seq % TOKEN_TILE == 0 and (n * TOP_K) % MOE_ROWS == 0

    mod_all = _ada_mod(c, w_ada, b_ada)
    cs, sn = _rope_tables(positions)
    xf = x.reshape(n, d)
    for l in range(depth):
        wts = _layer_weights(l, w_in, pool_w, w_uq, w_ukv, w_out, ssm_w_glu, router_w_group, router_b_group,
                             router_w_expert, router_b_expert)
        mod = mod_all[l]
        gn = out_norm_g[l]
        ypool, ussm, q, k, v = _mix_in(
            xf, mod, norm1_g[l].reshape(1, d), wts['win'], wts['poolw'], pool_scale[l].reshape(1, pw),
            gn[:pw].reshape(1, pw), q_norm_g[l].reshape(1, -1), kv_norm_g[l].reshape(1, -1),
            wts['wuq'], wts['wuqr'], wts['wukv'], cs, sn, seq=seq, heads=heads)
        m, bc, cc, tab = _ssm_tables(ssm_lam_re[l], ssm_lam_im[l], ssm_log_dt[l], ssm_b_re[l], ssm_b_im[l],
                                     ssm_c_re[l], ssm_c_im[l])
        yscan = _s5_scan(ussm, m, bc, cc, tab, batch=batch)
        ymla = _mla_attn(q, k, v, batch=batch, seq=seq, heads=heads)
        x1, h2, ridx, rw = _mix_out(
            xf, ypool, yscan, ussm, ymla, mod, ssm_d[l].reshape(1, sw), wts['wglu'],
            gn[pw:pw + sw].reshape(1, sw), gn[pw + sw:].reshape(1, -1), wts['wo'], norm2_g[l].reshape(1, d),
            wts['wrh'], wts['wrl'], wts['br'], seq=seq, n_exp=n_exp, n_grp=n_grp)
        dest, be, nv, nb = _route_plan(ridx, n_exp, MOE_ROWS)
        xs = _dispatch(nv, dest, h2, nb)
        ys = _experts(be, nv, xs, w_gate[l], w_up[l], w_down[l], nb)
        xf = _combine(dest, x1, rw, mod, final_g.reshape(1, d), ys, seq=seq, final=(l == depth - 1))
    return xf.reshape(batch, seq, d)
```

```python
import functools
import math

import jax
import jax.numpy as jnp
from jax import lax
from jax.experimental import pallas as pl
from jax.experimental.pallas import tpu as pltpu

F32 = jnp.float32
BF16 = jnp.bfloat16

POOL_WINDOWS = (2, 4, 8, 16)
SSM_GROUP = 16
QK_NOPE = 128
QK_ROPE = 64
V_HEAD = 128
ROPE_THETA = 10000.0
EPS = 1e-6
TOP_K = 2

LANES = 128
SUBLANES = 8
MXU_DIM = 256
VMEM_LIMIT = 56 * 1024 * 1024

SSM_CHUNK = 16
S5_ROWS = 4096
TOKEN_TILE = 256
ATTN_TB = 1024
ATTN_SUB = 2
MOE_ROWS = 256
NEG_BIG = -1e30


def _dot(a, b):
    return jnp.dot(a, b, preferred_element_type=F32)


def _split_bf16(a):
    hi = a.astype(BF16)
    lo = (a - hi.astype(F32)).astype(BF16)
    return hi, lo


def _rms(x, g):
    return x * lax.rsqrt(jnp.mean(x * x, axis=-1, keepdims=True) + EPS) * g


def _params(semantics):
    return pltpu.CompilerParams(dimension_semantics=semantics, vmem_limit_bytes=VMEM_LIMIT)


def _const_spec(shape):
    nd = len(shape)
    return pl.BlockSpec(shape, lambda *_: (0,) * nd, pipeline_mode=pl.Buffered(1))


def _ada_body(c_ref, w_ref, b_ref, o_ref):
    ch, cl = _split_bf16(c_ref[...])
    wh, wl = _split_bf16(w_ref[0])
    o_ref[0] = _dot(ch, wh) + _dot(cl, wh) + _dot(ch, wl) + b_ref[0]


def _ada_mod(c, w_ada, b_ada):
    depth, d, n6 = w_ada.shape
    b = c.shape[0]
    tn = 1024
    c_pad = jnp.zeros((SUBLANES, d), F32).at[:b].set(c)
    out = pl.pallas_call(
        _ada_body,
        grid=(depth, n6 // tn),
        in_specs=[pl.BlockSpec((SUBLANES, d), lambda l, j: (0, 0)),
                  pl.BlockSpec((1, d, tn), lambda l, j: (l, 0, j)),
                  pl.BlockSpec((1, 1, tn), lambda l, j: (l, 0, j))],
        out_specs=pl.BlockSpec((1, SUBLANES, tn), lambda l, j: (l, 0, j)),
        out_shape=jax.ShapeDtypeStruct((depth, SUBLANES, n6), F32),
        compiler_params=_params(("arbitrary", "arbitrary")),
        name="ada_mod",
    )(c_pad, w_ada, b_ada.reshape(depth, 1, n6))
    return out[:, :b].reshape(depth, b, 6, d)


def _rope_body(pos_ref, invf_ref, cos_ref, sin_ref):
    ang = pos_ref[...] * invf_ref[...]
    cos_ref[...] = jnp.cos(ang)
    sin_ref[...] = jnp.sin(ang)


def _rope_tables(positions):
    n = positions.size
    half = QK_ROPE // 2
    per_row = LANES // half
    inv_freq = jnp.power(ROPE_THETA, -jnp.arange(0, QK_ROPE, 2, dtype=F32) / QK_ROPE)
    pos = jnp.repeat(positions.reshape(n // per_row, per_row).astype(F32), half, axis=1)
    invf = jnp.tile(inv_freq, per_row).reshape(1, LANES)
    rows = n // per_row
    tr = min(rows, 1024)
    cos, sin = pl.pallas_call(
        _rope_body,
        grid=(rows // tr,),
        in_specs=[pl.BlockSpec((tr, LANES), lambda i: (i, 0)), pl.BlockSpec((1, LANES), lambda i: (0, 0))],
        out_specs=[pl.BlockSpec((tr, LANES), lambda i: (i, 0))] * 2,
        out_shape=[jax.ShapeDtypeStruct((rows, LANES), F32)] * 2,
        compiler_params=_params(("arbitrary",)),
        name="rope_tables",
    )(pos, invf)
    z = jnp.zeros((n, LANES - QK_ROPE), F32)
    cos, sin = cos.reshape(n, half), sin.reshape(n, half)
    return jnp.concatenate([cos, cos, z], axis=1), jnp.concatenate([sin, sin, z], axis=1)


def _mix_in_body(x_ref, mod_ref, g1_ref, win_ref, poolw_ref, pscale_ref, gnp_ref, qg_ref, kvg_ref,
                 wuq_ref, wuqr_ref, wukv_ref, cs_ref, sn_ref,
                 ypool_ref, ussm_ref, q_ref, k_ref, v_ref, ext_ref, *, tiles_per_batch, tm, heads, scale):
    tin = pl.program_id(0) % tiles_per_batch
    mod = mod_ref[0]
    sh_a, sc_a = mod[0:1], mod[1:2]
    h = _rms(x_ref[...], g1_ref[...]) * (1.0 + sc_a) + sh_a
    z = _dot(h.astype(BF16), win_ref[...])

    halo = max(POOL_WINDOWS)
    pw = len(POOL_WINDOWS) * LANES

    @pl.when(tin == 0)
    def _():
        ext_ref[0:halo, :] = jnp.zeros((halo, pw), F32)

    zp = z[:, 0:pw]
    ext_ref[halo:halo + tm, :] = zp
    t = tin * tm + lax.broadcasted_iota(jnp.int32, (tm, 1), 0)
    ys = []
    for gi, w in enumerate(POOL_WINDOWS):
        cols = slice(gi * LANES, (gi + 1) * LANES)
        tok = zp[:, cols]
        s = tok
        for j in range(1, w):
            s = s + ext_ref[halo - j:halo - j + tm, cols]
        cnt = jnp.minimum(t + 1, w).astype(F32)
        ys.append(_dot((s / cnt - tok).astype(BF16), poolw_ref[gi]))
    ypool = jnp.concatenate(ys, axis=1) * pscale_ref[...]
    ypool_ref[...] = _rms(ypool, gnp_ref[...]).astype(BF16)
    ext_ref[0:halo, :] = ext_ref[tm:tm + halo, :]

    ussm_ref[...] = z[:, pw:2 * pw]

    cs, sn = cs_ref[...], sn_ref[...]
    qn = _rms(z[:, 1024:1536], qg_ref[...]).astype(BF16)
    qm = _dot(qn, wuq_ref[...])
    qr = _dot(qn, wuqr_ref[...])
    cs_q, sn_q = cs * scale, sn * scale
    for hh in range(heads):
        o = hh * 2 * LANES
        q_ref[hh, :, 0:LANES] = (qm[:, o:o + LANES] * scale).astype(BF16)
        q_ref[hh, :, LANES:2 * LANES] = (qm[:, o + LANES:o + 2 * LANES] * cs_q
                                         + qr[:, hh * LANES:(hh + 1) * LANES] * sn_q).astype(BF16)
    kvn = _rms(z[:, 1536:1792], kvg_ref[...]).astype(BF16)
    kv = _dot(kvn, wukv_ref[...])
    kpe = (z[:, 1792:1920] * cs + z[:, 1920:2048] * sn).astype(BF16)
    for hh in range(heads):
        k_ref[hh, :, 0:LANES] = kv[:, hh * LANES:(hh + 1) * LANES].astype(BF16)
        k_ref[hh, :, LANES:2 * LANES] = kpe
    v_ref[...] = kv[:, heads * LANES:].astype(BF16)


def _mix_in(x, mod, g1, win, poolw, pscale, gnp, qg, kvg, wuq, wuqr, wukv, cs, sn, *, seq, heads):
    n, d = x.shape
    tm = TOKEN_TILE
    tpb = seq // tm
    pw = len(POOL_WINDOWS) * LANES
    halo = max(POOL_WINDOWS)
    scale = float((QK_NOPE + QK_ROPE) ** -0.5 * math.log2(math.e))
    row = lambda w: pl.BlockSpec((tm, w), lambda i: (i, 0))
    hd = pl.BlockSpec((heads, tm, 2 * LANES), lambda i: (0, i, 0))
    return pl.pallas_call(
        functools.partial(_mix_in_body, tiles_per_batch=tpb, tm=tm, heads=heads, scale=scale),
        grid=(n // tm,),
        in_specs=[row(d),
                  pl.BlockSpec((1, 6, d), lambda i: (i // tpb, 0, 0)),
                  _const_spec(g1.shape), _const_spec(win.shape), _const_spec(poolw.shape),
                  _const_spec(pscale.shape), _const_spec(gnp.shape), _const_spec(qg.shape),
                  _const_spec(kvg.shape), _const_spec(wuq.shape), _const_spec(wuqr.shape),
                  _const_spec(wukv.shape), row(LANES), row(LANES)],
        out_specs=[row(pw), row(pw), hd, hd, row(heads * V_HEAD)],
        out_shape=[jax.ShapeDtypeStruct((n, pw), BF16), jax.ShapeDtypeStruct((n, pw), F32),
                   jax.ShapeDtypeStruct((heads, n, 2 * LANES), BF16),
                   jax.ShapeDtypeStruct((heads, n, 2 * LANES), BF16),
                   jax.ShapeDtypeStruct((n, heads * V_HEAD), BF16)],
        scratch_shapes=[pltpu.VMEM((tm + halo, pw), F32)],
        compiler_params=_params(("arbitrary",)),
        name="mix_in",
    )(x, mod, g1, win, poolw, pscale, gnp, qg, kvg, wuq, wuqr, wukv, cs, sn)


def _s5_body(u_ref, m_ref, bc_ref, cc_ref, tab_ref, y_ref, xcat_ref, sre_ref, sim_ref, car_ref, *, nc):
    t_ch = SSM_CHUNK
    sw = tab_ref.shape[-1]

    @pl.when(pl.program_id(2) == 0)
    def _():
        car_ref[...] = jnp.zeros(car_ref.shape, F32)

    for t in range(t_ch):
        xcat_ref[:, t * LANES:(t + 1) * LANES] = u_ref[pl.ds(t, nc, stride=t_ch), :].astype(BF16)
    xcat = xcat_ref[...]
    y_in = _dot(xcat, m_ref[0])
    x = _dot(xcat, bc_ref[0])
    sre_ref[...] = x[:, 0:sw]
    sim_ref[...] = x[:, sw:2 * sw]
    first_row = lax.broadcasted_iota(jnp.int32, (SUBLANES, sw), 0) == 0

    def block(b, carry):
        cre, cim = carry
        r0 = pl.multiple_of(b * SUBLANES, SUBLANES)
        re = sre_ref[pl.ds(r0, SUBLANES), :]
        im = sim_ref[pl.ds(r0, SUBLANES), :]
        for n, k in enumerate((1, 2, 4)):
            tr, ti = tab_ref[0, 2 * n], tab_ref[0, 2 * n + 1]
            pre, pim = pltpu.roll(re, k, 0), pltpu.roll(im, k, 0)
            re, im = re + tr * pre - ti * pim, im + tr * pim + ti * pre
        pr, pi = tab_ref[0, 6], tab_ref[0, 7]
        ore = re + pr * cre - pi * cim
        oim = im + pr * cim + pi * cre
        sre_ref[pl.ds(r0, SUBLANES), :] = jnp.where(first_row, cre, pltpu.roll(ore, 1, 0))
        sim_ref[pl.ds(r0, SUBLANES), :] = jnp.where(first_row, cim, pltpu.roll(oim, 1, 0))
        last = SUBLANES - 1
        return (jnp.broadcast_to(ore[last:last + 1, :], (SUBLANES, sw)),
                jnp.broadcast_to(oim[last:last + 1, :], (SUBLANES, sw)))

    cre, cim = lax.fori_loop(0, nc // SUBLANES, block, (car_ref[0], car_ref[1]))
    car_ref[0] = cre
    car_ref[1] = cim
    sp = jnp.concatenate([sre_ref[...], sim_ref[...]], axis=1).astype(BF16)
    y = y_in + _dot(sp, cc_ref[0])
    for t in range(t_ch):
        y_ref[pl.ds(t, nc, stride=t_ch), :] = y[:, t * LANES:(t + 1) * LANES]


def _s5_scan(u, m, bc, cc, tab, *, batch):
    n, width = u.shape
    tiles = width // LANES
    seq = n // batch
    rs = min(S5_ROWS, seq)
    steps = seq // rs
    nc = rs // SSM_CHUNK
    sw = tab.shape[-1]
    kw = SSM_CHUNK * LANES
    blk = pl.BlockSpec((rs, LANES), lambda j, b, r: (b * steps + r, j))
    return pl.pallas_call(
        functools.partial(_s5_body, nc=nc),
        grid=(tiles, batch, steps),
        in_specs=[blk,
                  pl.BlockSpec((1, kw, kw), lambda j, b, r: (j, 0, 0)),
                  pl.BlockSpec((1, kw, 2 * sw), lambda j, b, r: (j, 0, 0)),
                  pl.BlockSpec((1, 2 * sw, kw), lambda j, b, r: (j, 0, 0)),
                  pl.BlockSpec((1, 8, SUBLANES, sw), lambda j, b, r: (j, 0, 0, 0))],
        out_specs=blk,
        out_shape=jax.ShapeDtypeStruct((n, width), F32),
        scratch_shapes=[pltpu.VMEM((nc, kw), BF16), pltpu.VMEM((nc, sw), F32), pltpu.VMEM((nc, sw), F32),
                        pltpu.VMEM((2, SUBLANES, sw), F32)],
        compiler_params=_params(("arbitrary", "arbitrary", "arbitrary")),
        name="s5_scan",
    )(u, m, bc, cc, tab)


def _ssm_tables(lam_re, lam_im, log_dt, b_re, b_im, c_re, c_im):
    hp = lax.Precision.HIGHEST
    g, p = lam_re.shape
    hh = b_re.shape[-1]
    t = SSM_CHUNK
    gq = LANES // hh
    tiles = g // gq
    dt = jnp.exp(log_dt)[:, None]
    mag = jnp.exp(lam_re * dt)
    ar, ai = mag * jnp.cos(lam_im * dt), mag * jnp.sin(lam_im * dt)
    den = lam_re * lam_re + lam_im * lam_im
    fr = ((ar - 1.0) * lam_re + ai * lam_im) / den
    fi = (ai * lam_re - (ar - 1.0) * lam_im) / den
    bbr = fr[..., None] * b_re - fi[..., None] * b_im
    bbi = fr[..., None] * b_im + fi[..., None] * b_re

    def powers(xr, xi, count):
        prs, pis = [jnp.ones_like(xr)], [jnp.zeros_like(xi)]
        for _ in range(count):
            prs.append(prs[-1] * xr - pis[-1] * xi)
            pis.append(prs[-2] * xi + pis[-1] * xr)
        return jnp.stack(prs), jnp.stack(pis)

    pr, pi = powers(ar, ai, t)
    car = c_re[None] * pr[:, :, None, :] - c_im[None] * pi[:, :, None, :]
    cai = c_im[None] * pr[:, :, None, :] + c_re[None] * pi[:, :, None, :]
    kern = (jnp.einsum('kgop,gpi->gkio', car[:t], bbr, precision=hp)
            - jnp.einsum('kgop,gpi->gkio', cai[:t], bbi, precision=hp))
    lag = jnp.arange(t)[None, :] - jnp.arange(t)[:, None]
    toep = jnp.where((lag >= 0)[None, :, :, None, None], kern[:, jnp.clip(lag, 0)], 0.0)
    eye = jnp.eye(gq, dtype=F32)
    kw = t * gq * hh
    toep = toep.reshape(tiles, gq, t, t, hh, hh)
    m = (toep.transpose(0, 2, 1, 4, 3, 5)[:, :, :, :, :, None, :]
         * eye[None, None, :, None, None, :, None]).reshape(tiles, kw, kw)
    rev = jnp.arange(t - 1, -1, -1)
    bcr = (pr[rev][..., None] * bbr[None] - pi[rev][..., None] * bbi[None]).transpose(1, 0, 3, 2)
    bci = (pr[rev][..., None] * bbi[None] + pi[rev][..., None] * bbr[None]).transpose(1, 0, 3, 2)
    bri = jnp.stack([bcr, bci], axis=3).reshape(tiles, gq, t, hh, 2, p)
    bc = (bri.transpose(0, 2, 1, 3, 4, 5)[:, :, :, :, :, None, :]
          * eye[None, None, :, None, None, :, None]).reshape(tiles, kw, 2 * gq * p)
    ccr = car[1:].transpose(1, 3, 0, 2)
    cci = -cai[1:].transpose(1, 3, 0, 2)
    cri = jnp.stack([ccr, cci], axis=1).reshape(tiles, gq, 2, p, t, hh)
    cc = (cri.transpose(0, 2, 1, 3, 4, 5)[:, :, :, :, :, None, :]
          * eye[None, None, :, None, None, :, None]).reshape(tiles, 2 * gq * p, kw)
    a16r, a16i = pr[t].reshape(tiles, gq * p), pi[t].reshape(tiles, gq * p)
    qr, qi = powers(a16r, a16i, SUBLANES)
    sub = jnp.arange(SUBLANES)
    tabs = []
    for k in (1, 2, 4):
        keep = (sub >= k).astype(F32)[None, :, None]
        tabs += [qr[k][:, None, :] * keep, qi[k][:, None, :] * keep]
    tabs += [qr[1:].transpose(1, 0, 2), qi[1:].transpose(1, 0, 2)]
    tab = jnp.stack(tabs, axis=1)
    return m.astype(BF16), bc.astype(BF16), cc.astype(BF16), tab


def _attn_body(q_ref, k_ref, v_ref, o_ref, *, tb, nsub):
    i = pl.program_id(1)
    qs = [q_ref[0, s * tb:(s + 1) * tb, :] for s in range(nsub)]
    diag = lax.broadcasted_iota(jnp.int32, (tb, tb), 1) <= lax.broadcasted_iota(jnp.int32, (tb, tb), 0)

    def load(j):
        k0 = pl.multiple_of(j * tb, tb)
        return k_ref[0, pl.ds(k0, tb), :], v_ref[pl.ds(k0, tb), :]

    def step(q, kb, vb, carry, masked):
        m, l, acc = carry
        s = lax.dot_general(q, kb, (((1,), (1,)), ((), ())), preferred_element_type=F32)
        if masked:
            s = jnp.where(diag, s, NEG_BIG)
        m_new = jnp.maximum(m, jnp.max(s, axis=-1, keepdims=True))
        p = jnp.exp2(s - m_new)
        alpha = jnp.exp2(m - m_new)
        l = alpha * l + jnp.sum(p, axis=-1, keepdims=True)
        acc = alpha * acc + _dot(p.astype(BF16), vb)
        return m_new, l, acc

    def body(j, carries):
        kb, vb = load(j)
        return tuple(step(qs[s], kb, vb, carries[s], False) for s in range(nsub))

    init = (jnp.full((tb, 1), NEG_BIG, F32), jnp.zeros((tb, 1), F32), jnp.zeros((tb, V_HEAD), F32))
    n_full = i * nsub
    carries = list(lax.fori_loop(0, n_full, body, (init,) * nsub))
    for d in range(nsub):
        kb, vb = load(n_full + d)
        for s in range(d, nsub):
            carries[s] = step(qs[s], kb, vb, carries[s], s == d)
    for s in range(nsub):
        _, l, acc = carries[s]
        o_ref[s * tb:(s + 1) * tb, :] = (acc / l).astype(BF16)


def _mla_attn(q, k, v, *, batch, seq, heads):
    tb, nsub = ATTN_TB, ATTN_SUB
    tq = tb * nsub
    nq = seq // tq
    n = batch * seq
    return pl.pallas_call(
        functools.partial(_attn_body, tb=tb, nsub=nsub),
        grid=(batch * heads, nq),
        in_specs=[pl.BlockSpec((1, tq, 2 * LANES), lambda bh, i: (bh % heads, (bh // heads) * nq + i, 0)),
                  pl.BlockSpec((1, seq, 2 * LANES), lambda bh, i: (bh % heads, bh // heads, 0)),
                  pl.BlockSpec((seq, V_HEAD), lambda bh, i: (bh // heads, bh % heads))],
        out_specs=pl.BlockSpec((tq, V_HEAD), lambda bh, i: ((bh // heads) * nq + i, bh % heads)),
        out_shape=jax.ShapeDtypeStruct((n, heads * V_HEAD), BF16),
        compiler_params=_params(("arbitrary", "arbitrary")),
        name="mla_attn",
    )(q, k, v)


def _gelu_tanh(x):
    return 0.5 * x * (1.0 + jnp.tanh(math.sqrt(2.0 / math.pi) * (x + 0.044715 * (x * x * x))))


def _mix_out_body(x_ref, yp_ref, ysc_ref, us_ref, ym_ref, mod_ref, dsk_ref, wglu_ref, gns_ref, gnm_ref,
                  wo_ref, g2_ref, wrh_ref, wrl_ref, br_ref,
                  x1_ref, h2_ref, ridx_ref, rw_ref, *, n_exp, n_grp):
    mod = mod_ref[0]
    g_a, sh_f, sc_f = mod[2:3], mod[3:4], mod[4:5]
    ys = _gelu_tanh(ysc_ref[...] + dsk_ref[...] * us_ref[...])
    ys = ys * jax.nn.sigmoid(_dot(ys.astype(BF16), wglu_ref[...]))
    ysn = _rms(ys, gns_ref[...]).astype(BF16)
    ymn = _rms(ym_ref[...].astype(F32), gnm_ref[...]).astype(BF16)
    wp, ws = yp_ref.shape[1], ysc_ref.shape[1]
    o = (_dot(yp_ref[...], wo_ref[0:wp, :]) + _dot(ysn, wo_ref[wp:wp + ws, :])
         + _dot(ymn, wo_ref[wp + ws:, :]))
    x1 = x_ref[...] + g_a * o
    x1_ref[...] = x1
    h2 = _rms(x1, g2_ref[...]) * (1.0 + sc_f) + sh_f
    h2_ref[...] = h2

    hh, hl = _split_bf16(h2)
    logits = _dot(hh, wrh_ref[...]) + _dot(hl, wrh_ref[...]) + _dot(hh, wrl_ref[...]) + br_ref[...]
    lane = lax.broadcasted_iota(jnp.int32, logits.shape, 1).astype(F32)
    far = float(2 * LANES)
    per = n_exp // n_grp
    gl = jnp.where(lane >= n_exp, jnp.where(lane < n_exp + n_grp, logits, NEG_BIG), NEG_BIG)
    gmax = jnp.max(gl, axis=-1, keepdims=True)
    gtop = jnp.min(jnp.where(gl == gmax, lane, far), axis=-1, keepdims=True) - n_exp
    g_w = 1.0 / jnp.sum(jnp.exp(gl - gmax), axis=-1, keepdims=True)
    lo = gtop * per
    el = jnp.where(lane >= lo, jnp.where(lane < lo + per, logits, NEG_BIG), NEG_BIG)
    m1 = jnp.max(el, axis=-1, keepdims=True)
    i1 = jnp.min(jnp.where(el == m1, lane, far), axis=-1, keepdims=True)
    el2 = jnp.where(lane == i1, NEG_BIG, el)
    m2 = jnp.max(el2, axis=-1, keepdims=True)
    i2 = jnp.min(jnp.where(el2 == m2, lane, far), axis=-1, keepdims=True)
    e21 = jnp.exp(m2 - m1)
    w1 = g_w / (1.0 + e21)
    w2 = g_w * e21 / (1.0 + e21)
    ridx_ref[...] = jnp.where(lane == 0.0, i1, jnp.where(lane == 1.0, i2, 0.0)).astype(jnp.int32)
    rw_ref[...] = jnp.where(lane == 0.0, w1, jnp.where(lane == 1.0, w2, 0.0))


def _mix_out(x, ypool, yscan, ussm, ymla, mod, dsk, wglu, gns, gnm, wo, g2, wrh, wrl, br, *, seq, n_exp, n_grp):
    n, d = x.shape
    tm = TOKEN_TILE
    tpb = seq // tm
    row = lambda w: pl.BlockSpec((tm, w), lambda i: (i, 0))
    return pl.pallas_call(
        functools.partial(_mix_out_body, n_exp=n_exp, n_grp=n_grp),
        grid=(n // tm,),
        in_specs=[row(d), row(ypool.shape[1]), row(yscan.shape[1]), row(ussm.shape[1]), row(ymla.shape[1]),
                  pl.BlockSpec((1, 6, d), lambda i: (i // tpb, 0, 0)),
                  _const_spec(dsk.shape), _const_spec(wglu.shape), _const_spec(gns.shape), _const_spec(gnm.shape),
                  _const_spec(wo.shape), _const_spec(g2.shape), _const_spec(wrh.shape), _const_spec(wrl.shape),
                  _const_spec(br.shape)],
        out_specs=[row(d), row(d), row(LANES), row(LANES)],
        out_shape=[jax.ShapeDtypeStruct((n, d), F32), jax.ShapeDtypeStruct((n, d), F32),
                   jax.ShapeDtypeStruct((n, LANES), jnp.int32), jax.ShapeDtypeStruct((n, LANES), F32)],
        compiler_params=_params(("arbitrary",)),
        name="mix_out",
    )(x, ypool, yscan, ussm, ymla, mod, dsk, wglu, gns, gnm, wo, g2, wrh, wrl, br)


def _route_plan(ridx, n_exp, blk):
    expert = ridx[:, :TOP_K].reshape(-1)
    a = expert.shape[0]
    onehot = (expert[:, None] == jnp.arange(n_exp, dtype=jnp.int32)[None, :]).astype(jnp.int32)
    csum = jnp.cumsum(onehot, axis=0)
    counts = csum[-1]
    padded = (counts + blk - 1) // blk * blk
    pad_end = jnp.cumsum(padded)
    pad_start = pad_end - padded
    dest = jnp.sum(onehot * (csum - 1 + pad_start[None, :]), axis=1).astype(jnp.int32)
    nb = a // blk + n_exp
    bstart = jnp.arange(nb, dtype=jnp.int32) * blk
    be = jnp.minimum(jnp.searchsorted(pad_end, bstart, side='right'), n_exp - 1).astype(jnp.int32)
    nv = jnp.clip(counts[be] - (bstart - pad_start[be]), 0, blk).astype(jnp.int32)
    return dest, be, nv, nb


def _dispatch_body(nv_ref, dest_ref, h_ref, xs_hbm, zero_ref, sem, zsem, *, tm, blk, nb):
    def zero_copy(j):
        return pltpu.make_async_copy(zero_ref, xs_hbm.at[pl.ds(j * blk, blk)], zsem)

    @pl.when(pl.program_id(0) == 0)
    def _():
        zero_ref[...] = jnp.zeros(zero_ref.shape, zero_ref.dtype)

        def fill(j, carry):
            @pl.when(nv_ref[j] < blk)
            def _():
                zero_copy(j).start()
            return carry

        def drain(j, carry):
            @pl.when(nv_ref[j] < blk)
            def _():
                zero_copy(j).wait()
            return carry

        lax.fori_loop(0, nb, fill, 0)
        lax.fori_loop(0, nb, drain, 0)

    def issue(r, carry):
        for k in range(TOP_K):
            pltpu.make_async_copy(h_ref.at[pl.ds(r, 1)], xs_hbm.at[pl.ds(dest_ref[TOP_K * r + k], 1)], sem).start()
        return carry

    lax.fori_loop(0, tm, issue, 0)
    for k in range(TOP_K):
        pltpu.make_async_copy(h_ref, xs_hbm.at[pl.ds(0, tm)], sem).wait()


def _dispatch(nv, dest, h2, nb):
    n, d = h2.shape
    tm = TOKEN_TILE
    blk = MOE_ROWS
    grid_spec = pltpu.PrefetchScalarGridSpec(
        num_scalar_prefetch=1,
        grid=(n // tm,),
        in_specs=[pl.BlockSpec((TOP_K * tm,), lambda i, nv: (i,), memory_space=pltpu.SMEM),
                  pl.BlockSpec((tm, d), lambda i, nv: (i, 0))],
        out_specs=pl.BlockSpec(memory_space=pl.ANY),
        scratch_shapes=[pltpu.VMEM((blk, d), h2.dtype), pltpu.SemaphoreType.DMA(()), pltpu.SemaphoreType.DMA(())],
    )
    return pl.pallas_call(
        functools.partial(_dispatch_body, tm=tm, blk=blk, nb=nb),
        grid_spec=grid_spec,
        out_shape=jax.ShapeDtypeStruct((nb * blk, d), h2.dtype),
        compiler_params=pltpu.CompilerParams(dimension_semantics=("arbitrary",), vmem_limit_bytes=VMEM_LIMIT,
                                             has_side_effects=True),
        name="moe_dispatch",
    )(nv, dest, h2)


def _expert_body(be_ref, nv_ref, xs_ref, w1_ref, w3_ref, w2_ref, ys_ref, w1b, w3b, w2b, *, blk):
    j = pl.program_id(0)
    e = be_ref[j]
    prev = be_ref[jnp.maximum(j - 1, 0)]

    @pl.when(jnp.logical_or(j == 0, e != prev))
    def _():
        w1b[...] = w1_ref[0].astype(BF16)
        w3b[...] = w3_ref[0].astype(BF16)
        w2b[...] = w2_ref[0].astype(BF16)

    nv = nv_ref[j]

    @pl.when(nv > 0)
    def _():
        x = xs_ref[...].astype(BF16)
        a = _dot(x, w1b[...])
        b = _dot(x, w3b[...])
        ys_ref[...] = _dot((a * jax.nn.sigmoid(a) * b).astype(BF16), w2b[...])

    @pl.when(nv == 0)
    def _():
        ys_ref[...] = jnp.zeros(ys_ref.shape, F32)


def _experts(be, nv, xs, w1, w3, w2, nb):
    blk = MOE_ROWS
    _, d, de = w1.shape
    grid_spec = pltpu.PrefetchScalarGridSpec(
        num_scalar_prefetch=2,
        grid=(nb,),
        in_specs=[pl.BlockSpec((blk, d), lambda j, be, nv: (j, 0)),
                  pl.BlockSpec((1, d, de), lambda j, be, nv: (be[j], 0, 0)),
                  pl.BlockSpec((1, d, de), lambda j, be, nv: (be[j], 0, 0)),
                  pl.BlockSpec((1, de, d), lambda j, be, nv: (be[j], 0, 0))],
        out_specs=pl.BlockSpec((blk, d), lambda j, be, nv: (j, 0)),
        scratch_shapes=[pltpu.VMEM((d, de), BF16), pltpu.VMEM((d, de), BF16), pltpu.VMEM((de, d), BF16)],
    )
    return pl.pallas_call(
        functools.partial(_expert_body, blk=blk),
        grid_spec=grid_spec,
        out_shape=jax.ShapeDtypeStruct((nb * blk, d), F32),
        compiler_params=_params(("arbitrary",)),
        name="moe_experts",
    )(be, nv, xs, w1, w3, w2)


def _combine_body(dest_ref, x1_ref, rw_ref, mod_ref, fg_ref, ys_hbm, out_ref, buf, sem, *, tm, final):
    def issue(r, carry):
        for k in range(TOP_K):
            pltpu.make_async_copy(ys_hbm.at[pl.ds(dest_ref[TOP_K * r + k], 1)], buf.at[k, pl.ds(r, 1)], sem).start()
        return carry

    lax.fori_loop(0, tm, issue, 0)
    for k in range(TOP_K):
        pltpu.make_async_copy(ys_hbm.at[pl.ds(0, tm)], buf.at[k], sem).wait()
    w = rw_ref[...]
    y = buf[0] * w[:, 0:1] + buf[1] * w[:, 1:2]
    x2 = x1_ref[...] + mod_ref[0][5:6] * y
    out_ref[...] = _rms(x2, fg_ref[...]) if final else x2


def _combine(dest, x1, rw, mod, fg, ys, *, seq, final):
    n, d = x1.shape
    tm = TOKEN_TILE
    tpb = seq // tm
    row = lambda w: pl.BlockSpec((tm, w), lambda i: (i, 0))
    return pl.pallas_call(
        functools.partial(_combine_body, tm=tm, final=final),
        grid=(n // tm,),
        in_specs=[pl.BlockSpec((TOP_K * tm,), lambda i: (i,), memory_space=pltpu.SMEM),
                  row(d), row(LANES),
                  pl.BlockSpec((1, 6, d), lambda i: (i // tpb, 0, 0)),
                  pl.BlockSpec((1, d), lambda i: (0, 0)),
                  pl.BlockSpec(memory_space=pl.ANY)],
        out_specs=row(d),
        out_shape=jax.ShapeDtypeStruct((n, d), F32),
        scratch_shapes=[pltpu.VMEM((TOP_K, tm, d), F32), pltpu.SemaphoreType.DMA(())],
        compiler_params=_params(("arbitrary",)),
        name="moe_combine",
    )(dest, x1, rw, mod, fg, ys)


def _rot_half_cols(w):
    half = w.shape[-1] // 2
    return jnp.concatenate([-w[..., half:], w[..., :half]], axis=-1)


def _layer_weights(l, w_in, pool_w, w_uq, w_ukv, w_out, ssm_w_glu, router_w_group, router_b_group,
                   router_w_expert, router_b_expert):
    d = w_in.shape[1]
    heads = w_uq.shape[2]
    wi = w_in[l]
    kpe = wi[:, -QK_ROPE:]
    pad = jnp.zeros((d, LANES - QK_ROPE), F32)
    win = jnp.concatenate([wi[:, :-QK_ROPE], kpe, pad, _rot_half_cols(kpe), pad], axis=1).astype(BF16)
    uq = w_uq[l]
    r = uq.shape[0]
    zq = jnp.zeros((r, heads, LANES - QK_ROPE), F32)
    wuq = jnp.concatenate([uq, zq], axis=-1).reshape(r, heads * 2 * LANES).astype(BF16)
    wuqr = jnp.concatenate([_rot_half_cols(uq[..., QK_NOPE:]), zq], axis=-1).reshape(r, heads * LANES).astype(BF16)
    ukv = w_ukv[l]
    wukv = jnp.concatenate([ukv[..., :QK_NOPE].reshape(ukv.shape[0], -1),
                            ukv[..., QK_NOPE:].reshape(ukv.shape[0], -1)], axis=1).astype(BF16)
    n_grp = router_w_group.shape[-1]
    n_exp = router_w_expert.shape[-1]
    wr = jnp.concatenate([router_w_expert[l], router_w_group[l],
                          jnp.zeros((d, LANES - n_exp - n_grp), F32)], axis=1)
    wrh, wrl = _split_bf16(wr)
    br = jnp.concatenate([router_b_expert[l], router_b_group[l],
                          jnp.zeros((LANES - n_exp - n_grp,), F32)]).reshape(1, LANES)
    return dict(win=win, poolw=pool_w[l].astype(BF16), wuq=wuq, wuqr=wuqr, wukv=wukv,
                wo=w_out[l].astype(BF16), wglu=ssm_w_glu[l].astype(BF16), wrh=wrh, wrl=wrl, br=br)


def kernel(x, c, positions, w_ada, b_ada, norm1_g, w_in, pool_w, pool_scale, ssm_lam_re, ssm_lam_im, ssm_log_dt, ssm_b_re, ssm_b_im, ssm_c_re, ssm_c_im, ssm_d, ssm_w_glu, q_norm_g, kv_norm_g, w_uq, w_ukv, out_norm_g, w_out, norm2_g, router_w_group, router_b_group, router_w_expert, router_b_expert, w_gate, w_up, w_down, final_g):
    batch, seq, d = x.shape
    depth = w_ada.shape[0]
    n = batch * seq
    heads = w_uq.shape[2]
    n_grp = router_w_group.shape[-1]
    n_exp = router_w_expert.shape[-1]
    pw = pool_w.shape[1] * pool_w.shape[2]
    sw = ssm_d.shape[-1]
    assert SSM_CHUNK * SSM_GROUP == MXU_DIM and pw == len(POOL_WINDOWS) * LANES and sw % LANES == 0
    assert seq % (ATTN_TB * ATTN_SUB) == 0 and seq % TOKEN_TILE == 0 and (n * TOP_K) % MOE_ROWS == 0

    mod_all = _ada_mod(c, w_ada, b_ada)
    cs, sn = _rope_tables(positions)
    xf = x.reshape(n, d)
    for l in range(depth):
        wts = _layer_weights(l, w_in, pool_w, w_uq, w_ukv, w_out, ssm_w_glu, router_w_group, router_b_group,
                             router_w_expert, router_b_expert)
        mod = mod_all[l]
        gn = out_norm_g[l]
        ypool, ussm, q, k, v = _mix_in(
            xf, mod, norm1_g[l].reshape(1, d), wts['win'], wts['poolw'], pool_scale[l].reshape(1, pw),
            gn[:pw].reshape(1, pw), q_norm_g[l].reshape(1, -1), kv_norm_g[l].reshape(1, -1),
            wts['wuq'], wts['wuqr'], wts['wukv'], cs, sn, seq=seq, heads=heads)
        m, bc, cc, tab = _ssm_tables(ssm_lam_re[l], ssm_lam_im[l], ssm_log_dt[l], ssm_b_re[l], ssm_b_im[l],
                                     ssm_c_re[l], ssm_c_im[l])
        yscan = _s5_scan(ussm, m, bc, cc, tab, batch=batch)
        ymla = _mla_attn(q, k, v, batch=batch, seq=seq, heads=heads)
        x1, h2, ridx, rw = _mix_out(
            xf, ypool, yscan, ussm, ymla, mod, ssm_d[l].reshape(1, sw), wts['wglu'],
            gn[pw:pw + sw].reshape(1, sw), gn[pw + sw:].reshape(1, -1), wts['wo'], norm2_g[l].reshape(1, d),
            wts['wrh'], wts['wrl'], wts['br'], seq=seq, n_exp=n_exp, n_grp=n_grp)
        dest, be, nv, nb = _route_plan(ridx, n_exp, MOE_ROWS)
        xs = _dispatch(nv, dest, h2, nb)
        ys = _experts(be, nv, xs, w_gate[l], w_up[l], w_down[l], nb)
        xf = _combine(dest, x1, rw, mod, final_g.reshape(1, d), ys, seq=seq, final=(l == depth - 1))
    return xf.reshape(batch, seq, d)
```

```python
import functools
import math

import jax
import jax.numpy as jnp
from jax import lax
from jax.experimental import pallas as pl
from jax.experimental.pallas import tpu as pltpu

F32 = jnp.float32
BF16 = jnp.bfloat16

POOL_WINDOWS = (2, 4, 8, 16)
SSM_GROUP = 16
QK_NOPE = 128
QK_ROPE = 64
V_HEAD = 128
ROPE_THETA = 10000.0
EPS = 1e-6
TOP_K = 2

LANES = 128
SUBLANES = 8
MXU_DIM = 256
VMEM_LIMIT = 56 * 1024 * 1024

SSM_CHUNK = 16
S5_ROWS = 4096
TOKEN_TILE = 256
ATTN_TB = 1024
ATTN_SUB = 2
MOE_ROWS = 256
NEG_BIG = -1e30


def _dot(a, b):
    return jnp.dot(a, b, preferred_element_type=F32)


def _split_bf16(a):
    hi = a.astype(BF16)
    lo = (a - hi.astype(F32)).astype(BF16)
    return hi, lo


def _rms(x, g):
    return x * lax.rsqrt(jnp.mean(x * x, axis=-1, keepdims=True) + EPS) * g


def _params(semantics):
    return pltpu.CompilerParams(dimension_semantics=semantics, vmem_limit_bytes=VMEM_LIMIT)


def _const_spec(shape):
    nd = len(shape)
    return pl.BlockSpec(shape, lambda *_: (0,) * nd, pipeline_mode=pl.Buffered(1))


def _ada_body(c_ref, w_ref, b_ref, o_ref):
    ch, cl = _split_bf16(c_ref[...])
    wh, wl = _split_bf16(w_ref[0])
    o_ref[0] = _dot(ch, wh) + _dot(cl, wh) + _dot(ch, wl) + b_ref[0]


def _ada_mod(c, w_ada, b_ada):
    depth, d, n6 = w_ada.shape
    b = c.shape[0]
    tn = 1024
    c_pad = jnp.zeros((SUBLANES, d), F32).at[:b].set(c)
    out = pl.pallas_call(
        _ada_body,
        grid=(depth, n6 // tn),
        in_specs=[pl.BlockSpec((SUBLANES, d), lambda l, j: (0, 0)),
                  pl.BlockSpec((1, d, tn), lambda l, j: (l, 0, j)),
                  pl.BlockSpec((1, 1, tn), lambda l, j: (l, 0, j))],
        out_specs=pl.BlockSpec((1, SUBLANES, tn), lambda l, j: (l, 0, j)),
        out_shape=jax.ShapeDtypeStruct((depth, SUBLANES, n6), F32),
        compiler_params=_params(("arbitrary", "arbitrary")),
        name="ada_mod",
    )(c_pad, w_ada, b_ada.reshape(depth, 1, n6))
    return out[:, :b].reshape(depth, b, 6, d)


def _rope_body(pos_ref, invf_ref, cos_ref, sin_ref):
    ang = pos_ref[...] * invf_ref[...]
    cos_ref[...] = jnp.cos(ang)
    sin_ref[...] = jnp.sin(ang)


def _rope_tables(positions):
    n = positions.size
    half = QK_ROPE // 2
    per_row = LANES // half
    inv_freq = jnp.power(ROPE_THETA, -jnp.arange(0, QK_ROPE, 2, dtype=F32) / QK_ROPE)
    pos = jnp.repeat(positions.reshape(n // per_row, per_row).astype(F32), half, axis=1)
    invf = jnp.tile(inv_freq, per_row).reshape(1, LANES)
    rows = n // per_row
    tr = min(rows, 1024)
    cos, sin = pl.pallas_call(
        _rope_body,
        grid=(rows // tr,),
        in_specs=[pl.BlockSpec((tr, LANES), lambda i: (i, 0)), pl.BlockSpec((1, LANES), lambda i: (0, 0))],
        out_specs=[pl.BlockSpec((tr, LANES), lambda i: (i, 0))] * 2,
        out_shape=[jax.ShapeDtypeStruct((rows, LANES), F32)] * 2,
        compiler_params=_params(("arbitrary",)),
        name="rope_tables",
    )(pos, invf)
    z = jnp.zeros((n, LANES - QK_ROPE), F32)
    cos, sin = cos.reshape(n, half), sin.reshape(n, half)
    return jnp.concatenate([cos, cos, z], axis=1), jnp.concatenate([sin, sin, z], axis=1)


def _mix_in_body(x_ref, mod_ref, g1_ref, win_ref, poolw_ref, pscale_ref, gnp_ref, qg_ref, kvg_ref,
                 wuq_ref, wuqr_ref, wukv_ref, cs_ref, sn_ref,
                 ypool_ref, ussm_ref, q_ref, k_ref, v_ref, ext_ref, *, tiles_per_batch, tm, heads, scale):
    tin = pl.program_id(0) % tiles_per_batch
    mod = mod_ref[0]
    sh_a, sc_a = mod[0:1], mod[1:2]
    h = _rms(x_ref[...], g1_ref[...]) * (1.0 + sc_a) + sh_a
    z = _dot(h.astype(BF16), win_ref[...])

    halo = max(POOL_WINDOWS)
    pw = len(POOL_WINDOWS) * LANES

    @pl.when(tin == 0)
    def _():
        ext_ref[0:halo, :] = jnp.zeros((halo, pw), F32)

    zp = z[:, 0:pw]
    ext_ref[halo:halo + tm, :] = zp
    t = tin * tm + lax.broadcasted_iota(jnp.int32, (tm, 1), 0)
    ys = []
    for gi, w in enumerate(POOL_WINDOWS):
        cols = slice(gi * LANES, (gi + 1) * LANES)
        tok = zp[:, cols]
        s = tok
        for j in range(1, w):
            s = s + ext_ref[halo - j:halo - j + tm, cols]
        cnt = jnp.minimum(t + 1, w).astype(F32)
        ys.append(_dot((s / cnt - tok).astype(BF16), poolw_ref[gi]))
    ypool = jnp.concatenate(ys, axis=1) * pscale_ref[...]
    ypool_ref[...] = _rms(ypool, gnp_ref[...]).astype(BF16)
    ext_ref[0:halo, :] = ext_ref[tm:tm + halo, :]

    ussm_ref[...] = z[:, pw:2 * pw]

    cs, sn = cs_ref[...], sn_ref[...]
    qn = _rms(z[:, 1024:1536], qg_ref[...]).astype(BF16)
    qm = _dot(qn, wuq_ref[...])
    qr = _dot(qn, wuqr_ref[...])
    cs_q, sn_q = cs * scale, sn * scale
    for hh in range(heads):
        o = hh * 2 * LANES
        q_ref[hh, :, 0:LANES] = (qm[:, o:o + LANES] * scale).astype(BF16)
        q_ref[hh, :, LANES:2 * LANES] = (qm[:, o + LANES:o + 2 * LANES] * cs_q
                                         + qr[:, hh * LANES:(hh + 1) * LANES] * sn_q).astype(BF16)
    kvn = _rms(z[:, 1536:1792], kvg_ref[...]).astype(BF16)
    kv = _dot(kvn, wukv_ref[...])
    kpe = (z[:, 1792:1920] * cs + z[:, 1920:2048] * sn).astype(BF16)
    for hh in range(heads):
        k_ref[hh, :, 0:LANES] = kv[:, hh * LANES:(hh + 1) * LANES].astype(BF16)
        k_ref[hh, :, LANES:2 * LANES] = kpe
    v_ref[...] = kv[:, heads * LANES:].astype(BF16)


def _mix_in(x, mod, g1, win, poolw, pscale, gnp, qg, kvg, wuq, wuqr, wukv, cs, sn, *, seq, heads):
    n, d = x.shape
    tm = TOKEN_TILE
    tpb = seq // tm
    pw = len(POOL_WINDOWS) * LANES
    halo = max(POOL_WINDOWS)
    scale = float((QK_NOPE + QK_ROPE) ** -0.5 * math.log2(math.e))
    row = lambda w: pl.BlockSpec((tm, w), lambda i: (i, 0))
    hd = pl.BlockSpec((heads, tm, 2 * LANES), lambda i: (0, i, 0))
    return pl.pallas_call(
        functools.partial(_mix_in_body, tiles_per_batch=tpb, tm=tm, heads=heads, scale=scale),
        grid=(n // tm,),
        in_specs=[row(d),
                  pl.BlockSpec((1, 6, d), lambda i: (i // tpb, 0, 0)),
                  _const_spec(g1.shape), _const_spec(win.shape), _const_spec(poolw.shape),
                  _const_spec(pscale.shape), _const_spec(gnp.shape), _const_spec(qg.shape),
                  _const_spec(kvg.shape), _const_spec(wuq.shape), _const_spec(wuqr.shape),
                  _const_spec(wukv.shape), row(LANES), row(LANES)],
        out_specs=[row(pw), row(pw), hd, hd, row(heads * V_HEAD)],
        out_shape=[jax.ShapeDtypeStruct((n, pw), BF16), jax.ShapeDtypeStruct((n, pw), F32),
                   jax.ShapeDtypeStruct((heads, n, 2 * LANES), BF16),
                   jax.ShapeDtypeStruct((heads, n, 2 * LANES), BF16),
                   jax.ShapeDtypeStruct((n, heads * V_HEAD), BF16)],
        scratch_shapes=[pltpu.VMEM((tm + halo, pw), F32)],
        compiler_params=_params(("arbitrary",)),
        name="mix_in",
    )(x, mod, g1, win, poolw, pscale, gnp, qg, kvg, wuq, wuqr, wukv, cs, sn)


def _s5_body(u_ref, m_ref, bc_ref, cc_ref, tab_ref, y_ref, xcat_ref, sre_ref, sim_ref, car_ref, *, nc):
    t_ch = SSM_CHUNK
    sw = tab_ref.shape[-1]

    @pl.when(pl.program_id(2) == 0)
    def _():
        car_ref[...] = jnp.zeros(car_ref.shape, F32)

    for t in range(t_ch):
        xcat_ref[:, t * LANES:(t + 1) * LANES] = u_ref[pl.ds(t, nc, stride=t_ch), :].astype(BF16)
    xcat = xcat_ref[...]
    y_in = _dot(xcat, m_ref[0])
    x = _dot(xcat, bc_ref[0])
    sre_ref[...] = x[:, 0:sw]
    sim_ref[...] = x[:, sw:2 * sw]
    first_row = lax.broadcasted_iota(jnp.int32, (SUBLANES, sw), 0) == 0

    def block(b, carry):
        cre, cim = carry
        r0 = pl.multiple_of(b * SUBLANES, SUBLANES)
        re = sre_ref[pl.ds(r0, SUBLANES), :]
        im = sim_ref[pl.ds(r0, SUBLANES), :]
        for n, k in enumerate((1, 2, 4)):
            tr, ti = tab_ref[0, 2 * n], tab_ref[0, 2 * n + 1]
            pre, pim = pltpu.roll(re, k, 0), pltpu.roll(im, k, 0)
            re, im = re + tr * pre - ti * pim, im + tr * pim + ti * pre
        pr, pi = tab_ref[0, 6], tab_ref[0, 7]
        ore = re + pr * cre - pi * cim
        oim = im + pr * cim + pi * cre
        sre_ref[pl.ds(r0, SUBLANES), :] = jnp.where(first_row, cre, pltpu.roll(ore, 1, 0))
        sim_ref[pl.ds(r0, SUBLANES), :] = jnp.where(first_row, cim, pltpu.roll(oim, 1, 0))
        last = SUBLANES - 1
        return (jnp.broadcast_to(ore[last:last + 1, :], (SUBLANES, sw)),
                jnp.broadcast_to(oim[last:last + 1, :], (SUBLANES, sw)))

    cre, cim = lax.fori_loop(0, nc // SUBLANES, block, (car_ref[0], car_ref[1]))
    car_ref[0] = cre
    car_ref[1] = cim
    sp = jnp.concatenate([sre_ref[...], sim_ref[...]], axis=1).astype(BF16)
    y = y_in + _dot(sp, cc_ref[0])
    for t in range(t_ch):
        y_ref[pl.ds(t, nc, stride=t_ch), :] = y[:, t * LANES:(t + 1) * LANES]


def _s5_scan(u, m, bc, cc, tab, *, batch):
    n, width = u.shape
    tiles = width // LANES
    seq = n // batch
    rs = min(S5_ROWS, seq)
    steps = seq // rs
    nc = rs // SSM_CHUNK
    sw = tab.shape[-1]
    kw = SSM_CHUNK * LANES
    blk = pl.BlockSpec((rs, LANES), lambda j, b, r: (b * steps + r, j))
    return pl.pallas_call(
        functools.partial(_s5_body, nc=nc),
        grid=(tiles, batch, steps),
        in_specs=[blk,
                  pl.BlockSpec((1, kw, kw), lambda j, b, r: (j, 0, 0)),
                  pl.BlockSpec((1, kw, 2 * sw), lambda j, b, r: (j, 0, 0)),
                  pl.BlockSpec((1, 2 * sw, kw), lambda j, b, r: (j, 0, 0)),
                  pl.BlockSpec((1, 8, SUBLANES, sw), lambda j, b, r: (j, 0, 0, 0))],
        out_specs=blk,
        out_shape=jax.ShapeDtypeStruct((n, width), F32),
        scratch_shapes=[pltpu.VMEM((nc, kw), BF16), pltpu.VMEM((nc, sw), F32), pltpu.VMEM((nc, sw), F32),
                        pltpu.VMEM((2, SUBLANES, sw), F32)],
        compiler_params=_params(("arbitrary", "arbitrary", "arbitrary")),
        name="s5_scan",
    )(u, m, bc, cc, tab)


def _ssm_tables(lam_re, lam_im, log_dt, b_re, b_im, c_re, c_im):
    hp = lax.Precision.HIGHEST
    g, p = lam_re.shape
    hh = b_re.shape[-1]
    t = SSM_CHUNK
    gq = LANES // hh
    tiles = g // gq
    dt = jnp.exp(log_dt)[:, None]
    mag = jnp.exp(lam_re * dt)
    ar, ai = mag * jnp.cos(lam_im * dt), mag * jnp.sin(lam_im * dt)
    den = lam_re * lam_re + lam_im * lam_im
    fr = ((ar - 1.0) * lam_re + ai * lam_im) / den
    fi = (ai * lam_re - (ar - 1.0) * lam_im) / den
    bbr = fr[..., None] * b_re - fi[..., None] * b_im
    bbi = fr[..., None] * b_im + fi[..., None] * b_re

    def powers(xr, xi, count):
        prs, pis = [jnp.ones_like(xr)], [jnp.zeros_like(xi)]
        for _ in range(count):
            prs.append(prs[-1] * xr - pis[-1] * xi)
            pis.append(prs[-2] * xi + pis[-1] * xr)
        return jnp.stack(prs), jnp.stack(pis)

    pr, pi = powers(ar, ai, t)
    car = c_re[None] * pr[:, :, None, :] - c_im[None] * pi[:, :, None, :]
    cai = c_im[None] * pr[:, :, None, :] + c_re[None] * pi[:, :, None, :]
    kern = (jnp.einsum('kgop,gpi->gkio', car[:t], bbr, precision=hp)
            - jnp.einsum('kgop,gpi->gkio', cai[:t], bbi, precision=hp))
    eye = jnp.eye(gq, dtype=F32)
    kw = t * gq * hh
    kq = kern.reshape(tiles, gq, t, hh, hh).transpose(0, 2, 1, 3, 4)
    lag_blk = (kq[:, :, :, :, None, :] * eye[None, None, :, None, :, None]).reshape(tiles, t, LANES, LANES)
    lag = jnp.arange(t)[None, :] - jnp.arange(t)[:, None]
    toep = jnp.where((lag >= 0)[None, :, :, None, None], lag_blk.astype(BF16)[:, jnp.clip(lag, 0)], 0)
    m = toep.transpose(0, 1, 3, 2, 4).reshape(tiles, kw, kw)
    rev = jnp.arange(t - 1, -1, -1)
    bcr = (pr[rev][..., None] * bbr[None] - pi[rev][..., None] * bbi[None]).transpose(1, 0, 3, 2)
    bci = (pr[rev][..., None] * bbi[None] + pi[rev][..., None] * bbr[None]).transpose(1, 0, 3, 2)
    bri = jnp.stack([bcr, bci], axis=3).reshape(tiles, gq, t, hh, 2, p)
    bc = (bri.transpose(0, 2, 1, 3, 4, 5)[:, :, :, :, :, None, :]
          * eye[None, None, :, None, None, :, None]).reshape(tiles, kw, 2 * gq * p)
    ccr = car[1:].transpose(1, 3, 0, 2)
    cci = -cai[1:].transpose(1, 3, 0, 2)
    cri = jnp.stack([ccr, cci], axis=1).reshape(tiles, gq, 2, p, t, hh)
    cc = (cri.transpose(0, 2, 1, 3, 4, 5)[:, :, :, :, :, None, :]
          * eye[None, None, :, None, None, :, None]).reshape(tiles, 2 * gq * p, kw)
    a16r, a16i = pr[t].reshape(tiles, gq * p), pi[t].reshape(tiles, gq * p)
    qr, qi = powers(a16r, a16i, SUBLANES)
    sub = jnp.arange(SUBLANES)
    tabs = []
    for k in (1, 2, 4):
        keep = (sub >= k).astype(F32)[None, :, None]
        tabs += [qr[k][:, None, :] * keep, qi[k][:, None, :] * keep]
    tabs += [qr[1:].transpose(1, 0, 2), qi[1:].transpose(1, 0, 2)]
    tab = jnp.stack(tabs, axis=1)
    return m.astype(BF16), bc.astype(BF16), cc.astype(BF16), tab


def _attn_body(q_ref, k_ref, v_ref, o_ref, *, tb, nsub):
    i = pl.program_id(1)
    qs = [q_ref[0, s * tb:(s + 1) * tb, :] for s in range(nsub)]
    diag = lax.broadcasted_iota(jnp.int32, (tb, tb), 1) <= lax.broadcasted_iota(jnp.int32, (tb, tb), 0)

    def load(j):
        k0 = pl.multiple_of(j * tb, tb)
        return k_ref[0, pl.ds(k0, tb), :], v_ref[pl.ds(k0, tb), :]

    def step(q, kb, vb, carry, masked):
        m, l, acc = carry
        s = lax.dot_general(q, kb, (((1,), (1,)), ((), ())), preferred_element_type=F32)
        if masked:
            s = jnp.where(diag, s, NEG_BIG)
        m_new = jnp.maximum(m, jnp.max(s, axis=-1, keepdims=True))
        p = jnp.exp2(s - m_new)
        alpha = jnp.exp2(m - m_new)
        l = alpha * l + jnp.sum(p, axis=-1, keepdims=True)
        acc = alpha * acc + _dot(p.astype(BF16), vb)
        return m_new, l, acc

    def body(j, carries):
        kb, vb = load(j)
        return tuple(step(qs[s], kb, vb, carries[s], False) for s in range(nsub))

    init = (jnp.full((tb, 1), NEG_BIG, F32), jnp.zeros((tb, 1), F32), jnp.zeros((tb, V_HEAD), F32))
    n_full = i * nsub
    carries = list(lax.fori_loop(0, n_full, body, (init,) * nsub))
    for d in range(nsub):
        kb, vb = load(n_full + d)
        for s in range(d, nsub):
            carries[s] = step(qs[s], kb, vb, carries[s], s == d)
    for s in range(nsub):
        _, l, acc = carries[s]
        o_ref[s * tb:(s + 1) * tb, :] = (acc / l).astype(BF16)


def _mla_attn(q, k, v, *, batch, seq, heads):
    tb, nsub = ATTN_TB, ATTN_SUB
    tq = tb * nsub
    nq = seq // tq
    n = batch * seq
    return pl.pallas_call(
        functools.partial(_attn_body, tb=tb, nsub=nsub),
        grid=(batch * heads, nq),
        in_specs=[pl.BlockSpec((1, tq, 2 * LANES), lambda bh, i: (bh % heads, (bh // heads) * nq + i, 0)),
                  pl.BlockSpec((1, seq, 2 * LANES), lambda bh, i: (bh % heads, bh // heads, 0)),
                  pl.BlockSpec((seq, V_HEAD), lambda bh, i: (bh // heads, bh % heads))],
        out_specs=pl.BlockSpec((tq, V_HEAD), lambda bh, i: ((bh // heads) * nq + i, bh % heads)),
        out_shape=jax.ShapeDtypeStruct((n, heads * V_HEAD), BF16),
        compiler_params=_params(("arbitrary", "arbitrary")),
        name="mla_attn",
    )(q, k, v)


def _gelu_tanh(x):
    return 0.5 * x * (1.0 + jnp.tanh(math.sqrt(2.0 / math.pi) * (x + 0.044715 * (x * x * x))))


def _mix_out_body(x_ref, yp_ref, ysc_ref, us_ref, ym_ref, mod_ref, dsk_ref, wglu_ref, gns_ref, gnm_ref,
                  wo_ref, g2_ref, wrh_ref, wrl_ref, br_ref,
                  x1_ref, h2_ref, ridx_ref, rw_ref, *, n_exp, n_grp):
    mod = mod_ref[0]
    g_a, sh_f, sc_f = mod[2:3], mod[3:4], mod[4:5]
    ys = _gelu_tanh(ysc_ref[...] + dsk_ref[...] * us_ref[...])
    ys = ys * jax.nn.sigmoid(_dot(ys.astype(BF16), wglu_ref[...]))
    ysn = _rms(ys, gns_ref[...]).astype(BF16)
    ymn = _rms(ym_ref[...].astype(F32), gnm_ref[...]).astype(BF16)
    wp, ws = yp_ref.shape[1], ysc_ref.shape[1]
    o = (_dot(yp_ref[...], wo_ref[0:wp, :]) + _dot(ysn, wo_ref[wp:wp + ws, :])
         + _dot(ymn, wo_ref[wp + ws:, :]))
    x1 = x_ref[...] + g_a * o
    x1_ref[...] = x1
    h2 = _rms(x1, g2_ref[...]) * (1.0 + sc_f) + sh_f
    h2_ref[...] = h2

    hh, hl = _split_bf16(h2)
    logits = _dot(hh, wrh_ref[...]) + _dot(hl, wrh_ref[...]) + _dot(hh, wrl_ref[...]) + br_ref[...]
    lane = lax.broadcasted_iota(jnp.int32, logits.shape, 1).astype(F32)
    far = float(2 * LANES)
    per = n_exp // n_grp
    gl = jnp.where(lane >= n_exp, jnp.where(lane < n_exp + n_grp, logits, NEG_BIG), NEG_BIG)
    gmax = jnp.max(gl, axis=-1, keepdims=True)
    gtop = jnp.min(jnp.where(gl == gmax, lane, far), axis=-1, keepdims=True) - n_exp
    g_w = 1.0 / jnp.sum(jnp.exp(gl - gmax), axis=-1, keepdims=True)
    lo = gtop * per
    el = jnp.where(lane >= lo, jnp.where(lane < lo + per, logits, NEG_BIG), NEG_BIG)
    m1 = jnp.max(el, axis=-1, keepdims=True)
    i1 = jnp.min(jnp.where(el == m1, lane, far), axis=-1, keepdims=True)
    el2 = jnp.where(lane == i1, NEG_BIG, el)
    m2 = jnp.max(el2, axis=-1, keepdims=True)
    i2 = jnp.min(jnp.where(el2 == m2, lane, far), axis=-1, keepdims=True)
    e21 = jnp.exp(m2 - m1)
    w1 = g_w / (1.0 + e21)
    w2 = g_w * e21 / (1.0 + e21)
    ridx_ref[...] = jnp.where(lane == 0.0, i1, jnp.where(lane == 1.0, i2, 0.0)).astype(jnp.int32)
    rw_ref[...] = jnp.where(lane == 0.0, w1, jnp.where(lane == 1.0, w2, 0.0))


def _mix_out(x, ypool, yscan, ussm, ymla, mod, dsk, wglu, gns, gnm, wo, g2, wrh, wrl, br, *, seq, n_exp, n_grp):
    n, d = x.shape
    tm = TOKEN_TILE
    tpb = seq // tm
    row = lambda w: pl.BlockSpec((tm, w), lambda i: (i, 0))
    return pl.pallas_call(
        functools.partial(_mix_out_body, n_exp=n_exp, n_grp=n_grp),
        grid=(n // tm,),
        in_specs=[row(d), row(ypool.shape[1]), row(yscan.shape[1]), row(ussm.shape[1]), row(ymla.shape[1]),
                  pl.BlockSpec((1, 6, d), lambda i: (i // tpb, 0, 0)),
                  _const_spec(dsk.shape), _const_spec(wglu.shape), _const_spec(gns.shape), _const_spec(gnm.shape),
                  _const_spec(wo.shape), _const_spec(g2.shape), _const_spec(wrh.shape), _const_spec(wrl.shape),
                  _const_spec(br.shape)],
        out_specs=[row(d), row(d), row(LANES), row(LANES)],
        out_shape=[jax.ShapeDtypeStruct((n, d), F32), jax.ShapeDtypeStruct((n, d), F32),
                   jax.ShapeDtypeStruct((n, LANES), jnp.int32), jax.ShapeDtypeStruct((n, LANES), F32)],
        compiler_params=_params(("arbitrary",)),
        name="mix_out",
    )(x, ypool, yscan, ussm, ymla, mod, dsk, wglu, gns, gnm, wo, g2, wrh, wrl, br)


def _route_plan(ridx, n_exp, blk):
    expert = ridx[:, :TOP_K].reshape(-1)
    a = expert.shape[0]
    onehot = (expert[:, None] == jnp.arange(n_exp, dtype=jnp.int32)[None, :]).astype(jnp.int32)
    seg = 256
    oh3 = onehot.astype(F32).reshape(a // seg, seg, n_exp)
    intra = jnp.einsum('ij,bje->bie', jnp.tril(jnp.ones((seg, seg), F32)), oh3)
    tot = intra[:, -1, :]
    csum = (intra + (jnp.cumsum(tot, axis=0) - tot)[:, None, :]).reshape(a, n_exp).astype(jnp.int32)
    counts = csum[-1]
    padded = (counts + blk - 1) // blk * blk
    pad_end = jnp.cumsum(padded)
    pad_start = pad_end - padded
    dest = jnp.sum(onehot * (csum - 1 + pad_start[None, :]), axis=1).astype(jnp.int32)
    nb = a // blk + n_exp
    bstart = jnp.arange(nb, dtype=jnp.int32) * blk
    be = jnp.minimum(jnp.searchsorted(pad_end, bstart, side='right'), n_exp - 1).astype(jnp.int32)
    nv = jnp.clip(counts[be] - (bstart - pad_start[be]), 0, blk).astype(jnp.int32)
    return dest, be, nv, nb


def _dispatch_body(nv_ref, dest_ref, h_ref, xs_hbm, zero_ref, sem, zsem, *, tm, blk, nb):
    def zero_copy(j):
        return pltpu.make_async_copy(zero_ref, xs_hbm.at[pl.ds(j * blk, blk)], zsem)

    @pl.when(pl.program_id(0) == 0)
    def _():
        zero_ref[...] = jnp.zeros(zero_ref.shape, zero_ref.dtype)

        def fill(j, carry):
            @pl.when(nv_ref[j] < blk)
            def _():
                zero_copy(j).start()
            return carry

        def drain(j, carry):
            @pl.when(nv_ref[j] < blk)
            def _():
                zero_copy(j).wait()
            return carry

        lax.fori_loop(0, nb, fill, 0)
        lax.fori_loop(0, nb, drain, 0)

    def issue(r, carry):
        for k in range(TOP_K):
            pltpu.make_async_copy(h_ref.at[pl.ds(r, 1)], xs_hbm.at[pl.ds(dest_ref[TOP_K * r + k], 1)], sem).start()
        return carry

    lax.fori_loop(0, tm, issue, 0)
    for k in range(TOP_K):
        pltpu.make_async_copy(h_ref, xs_hbm.at[pl.ds(0, tm)], sem).wait()


def _dispatch(nv, dest, h2, nb):
    n, d = h2.shape
    tm = TOKEN_TILE
    blk = MOE_ROWS
    grid_spec = pltpu.PrefetchScalarGridSpec(
        num_scalar_prefetch=1,
        grid=(n // tm,),
        in_specs=[pl.BlockSpec((TOP_K * tm,), lambda i, nv: (i,), memory_space=pltpu.SMEM),
                  pl.BlockSpec((tm, d), lambda i, nv: (i, 0))],
        out_specs=pl.BlockSpec(memory_space=pl.ANY),
        scratch_shapes=[pltpu.VMEM((blk, d), h2.dtype), pltpu.SemaphoreType.DMA(()), pltpu.SemaphoreType.DMA(())],
    )
    return pl.pallas_call(
        functools.partial(_dispatch_body, tm=tm, blk=blk, nb=nb),
        grid_spec=grid_spec,
        out_shape=jax.ShapeDtypeStruct((nb * blk, d), h2.dtype),
        compiler_params=pltpu.CompilerParams(dimension_semantics=("arbitrary",), vmem_limit_bytes=VMEM_LIMIT,
                                             has_side_effects=True),
        name="moe_dispatch",
    )(nv, dest, h2)


def _expert_body(be_ref, nv_ref, xs_ref, w1_ref, w3_ref, w2_ref, ys_ref, w1b, w3b, w2b, *, blk):
    j = pl.program_id(0)
    e = be_ref[j]
    prev = be_ref[jnp.maximum(j - 1, 0)]

    @pl.when(jnp.logical_or(j == 0, e != prev))
    def _():
        w1b[...] = w1_ref[0, 0].astype(BF16)
        w3b[...] = w3_ref[0, 0].astype(BF16)
        w2b[...] = w2_ref[0, 0].astype(BF16)

    nv = nv_ref[j]

    @pl.when(nv > 0)
    def _():
        x = xs_ref[...].astype(BF16)
        a = _dot(x, w1b[...])
        b = _dot(x, w3b[...])
        ys_ref[...] = _dot((a * jax.nn.sigmoid(a) * b).astype(BF16), w2b[...])

    @pl.when(nv == 0)
    def _():
        ys_ref[...] = jnp.zeros(ys_ref.shape, F32)


def _experts(be, nv, xs, w1, w3, w2, nb, layer):
    blk = MOE_ROWS
    _, _, d, de = w1.shape
    grid_spec = pltpu.PrefetchScalarGridSpec(
        num_scalar_prefetch=2,
        grid=(nb,),
        in_specs=[pl.BlockSpec((blk, d), lambda j, be, nv: (j, 0)),
                  pl.BlockSpec((1, 1, d, de), lambda j, be, nv: (layer, be[j], 0, 0)),
                  pl.BlockSpec((1, 1, d, de), lambda j, be, nv: (layer, be[j], 0, 0)),
                  pl.BlockSpec((1, 1, de, d), lambda j, be, nv: (layer, be[j], 0, 0))],
        out_specs=pl.BlockSpec((blk, d), lambda j, be, nv: (j, 0)),
        scratch_shapes=[pltpu.VMEM((d, de), BF16), pltpu.VMEM((d, de), BF16), pltpu.VMEM((de, d), BF16)],
    )
    return pl.pallas_call(
        functools.partial(_expert_body, blk=blk),
        grid_spec=grid_spec,
        out_shape=jax.ShapeDtypeStruct((nb * blk, d), F32),
        compiler_params=_params(("arbitrary",)),
        name="moe_experts",
    )(be, nv, xs, w1, w3, w2)


def _combine_body(dest_ref, x1_ref, rw_ref, mod_ref, fg_ref, ys_hbm, out_ref, buf, sem, *, tm, final):
    def issue(r, carry):
        for k in range(TOP_K):
            pltpu.make_async_copy(ys_hbm.at[pl.ds(dest_ref[TOP_K * r + k], 1)], buf.at[k, pl.ds(r, 1)], sem).start()
        return carry

    lax.fori_loop(0, tm, issue, 0)
    for k in range(TOP_K):
        pltpu.make_async_copy(ys_hbm.at[pl.ds(0, tm)], buf.at[k], sem).wait()
    w = rw_ref[...]
    y = buf[0] * w[:, 0:1] + buf[1] * w[:, 1:2]
    x2 = x1_ref[...] + mod_ref[0][5:6] * y
    out_ref[...] = _rms(x2, fg_ref[...]) if final else x2


def _combine(dest, x1, rw, mod, fg, ys, *, seq, final):
    n, d = x1.shape
    tm = TOKEN_TILE
    tpb = seq // tm
    row = lambda w: pl.BlockSpec((tm, w), lambda i: (i, 0))
    return pl.pallas_call(
        functools.partial(_combine_body, tm=tm, final=final),
        grid=(n // tm,),
        in_specs=[pl.BlockSpec((TOP_K * tm,), lambda i: (i,), memory_space=pltpu.SMEM),
                  row(d), row(LANES),
                  pl.BlockSpec((1, 6, d), lambda i: (i // tpb, 0, 0)),
                  pl.BlockSpec((1, d), lambda i: (0, 0)),
                  pl.BlockSpec(memory_space=pl.ANY)],
        out_specs=row(d),
        out_shape=jax.ShapeDtypeStruct((n, d), F32),
        scratch_shapes=[pltpu.VMEM((TOP_K, tm, d), F32), pltpu.SemaphoreType.DMA(())],
        compiler_params=_params(("arbitrary",)),
        name="moe_combine",
    )(dest, x1, rw, mod, fg, ys)


def _rot_half_cols(w):
    half = w.shape[-1] // 2
    return jnp.concatenate([-w[..., half:], w[..., :half]], axis=-1)


def _layer_weights(l, w_in, pool_w, w_uq, w_ukv, w_out, ssm_w_glu, router_w_group, router_b_group,
                   router_w_expert, router_b_expert):
    d = w_in.shape[1]
    heads = w_uq.shape[2]
    wi = w_in[l]
    kpe = wi[:, -QK_ROPE:]
    pad = jnp.zeros((d, LANES - QK_ROPE), F32)
    win = jnp.concatenate([wi[:, :-QK_ROPE], kpe, pad, _rot_half_cols(kpe), pad], axis=1).astype(BF16)
    uq = w_uq[l]
    r = uq.shape[0]
    zq = jnp.zeros((r, heads, LANES - QK_ROPE), F32)
    wuq = jnp.concatenate([uq, zq], axis=-1).reshape(r, heads * 2 * LANES).astype(BF16)
    wuqr = jnp.concatenate([_rot_half_cols(uq[..., QK_NOPE:]), zq], axis=-1).reshape(r, heads * LANES).astype(BF16)
    ukv = w_ukv[l]
    wukv = jnp.concatenate([ukv[..., :QK_NOPE].reshape(ukv.shape[0], -1),
                            ukv[..., QK_NOPE:].reshape(ukv.shape[0], -1)], axis=1).astype(BF16)
    n_grp = router_w_group.shape[-1]
    n_exp = router_w_expert.shape[-1]
    wr = jnp.concatenate([router_w_expert[l], router_w_group[l],
                          jnp.zeros((d, LANES - n_exp - n_grp), F32)], axis=1)
    wrh, wrl = _split_bf16(wr)
    br = jnp.concatenate([router_b_expert[l], router_b_group[l],
                          jnp.zeros((LANES - n_exp - n_grp,), F32)]).reshape(1, LANES)
    return dict(win=win, poolw=pool_w[l].astype(BF16), wuq=wuq, wuqr=wuqr, wukv=wukv,
                wo=w_out[l].astype(BF16), wglu=ssm_w_glu[l].astype(BF16), wrh=wrh, wrl=wrl, br=br)


def kernel(x, c, positions, w_ada, b_ada, norm1_g, w_in, pool_w, pool_scale, ssm_lam_re, ssm_lam_im, ssm_log_dt, ssm_b_re, ssm_b_im, ssm_c_re, ssm_c_im, ssm_d, ssm_w_glu, q_norm_g, kv_norm_g, w_uq, w_ukv, out_norm_g, w_out, norm2_g, router_w_group, router_b_group, router_w_expert, router_b_expert, w_gate, w_up, w_down, final_g):
    batch, seq, d = x.shape
    depth = w_ada.shape[0]
    n = batch * seq
    heads = w_uq.shape[2]
    n_grp = router_w_group.shape[-1]
    n_exp = router_w_expert.shape[-1]
    pw = pool_w.shape[1] * pool_w.shape[2]
    sw = ssm_d.shape[-1]
    assert SSM_CHUNK * SSM_GROUP == MXU_DIM and pw == len(POOL_WINDOWS) * LANES and sw % LANES == 0
    assert seq % (ATTN_TB * ATTN_SUB) == 0 and seq % TOKEN_TILE == 0 and (n * TOP_K) % MOE_ROWS == 0

    mod_all = _ada_mod(c, w_ada, b_ada)
    cs, sn = _rope_tables(positions)
    xf = x.reshape(n, d)
    for l in range(depth):
        wts = _layer_weights(l, w_in, pool_w, w_uq, w_ukv, w_out, ssm_w_glu, router_w_group, router_b_group,
                             router_w_expert, router_b_expert)
        mod = mod_all[l]
        gn = out_norm_g[l]
        ypool, ussm, q, k, v = _mix_in(
            xf, mod, norm1_g[l].reshape(1, d), wts['win'], wts['poolw'], pool_scale[l].reshape(1, pw),
            gn[:pw].reshape(1, pw), q_norm_g[l].reshape(1, -1), kv_norm_g[l].reshape(1, -1),
            wts['wuq'], wts['wuqr'], wts['wukv'], cs, sn, seq=seq, heads=heads)
        m, bc, cc, tab = _ssm_tables(ssm_lam_re[l], ssm_lam_im[l], ssm_log_dt[l], ssm_b_re[l], ssm_b_im[l],
                                     ssm_c_re[l], ssm_c_im[l])
        yscan = _s5_scan(ussm, m, bc, cc, tab, batch=batch)
        ymla = _mla_attn(q, k, v, batch=batch, seq=seq, heads=heads)
        x1, h2, ridx, rw = _mix_out(
            xf, ypool, yscan, ussm, ymla, mod, ssm_d[l].reshape(1, sw), wts['wglu'],
            gn[pw:pw + sw].reshape(1, sw), gn[pw + sw:].reshape(1, -1), wts['wo'], norm2_g[l].reshape(1, d),
            wts['wrh'], wts['wrl'], wts['br'], seq=seq, n_exp=n_exp, n_grp=n_grp)
        dest, be, nv, nb = _route_plan(ridx, n_exp, MOE_ROWS)
        xs = _dispatch(nv, dest, h2, nb)
        ys = _experts(be, nv, xs, w_gate, w_up, w_down, nb, l)
        xf = _combine(dest, x1, rw, mod, final_g.reshape(1, d), ys, seq=seq, final=(l == depth - 1))
    return xf.reshape(batch, seq, d)
```

```python
import functools
import math

import jax
import jax.numpy as jnp
from jax import lax
from jax.experimental import pallas as pl
from jax.experimental.pallas import tpu as pltpu

F32 = jnp.float32
BF16 = jnp.bfloat16

POOL_WINDOWS = (2, 4, 8, 16)
SSM_GROUP = 16
QK_NOPE = 128
QK_ROPE = 64
V_HEAD = 128
ROPE_THETA = 10000.0
EPS = 1e-6
TOP_K = 2

LANES = 128
SUBLANES = 8
MXU_DIM = 256
VMEM_LIMIT = 56 * 1024 * 1024

SSM_CHUNK = 16
S5_ROWS = 4096
TOKEN_TILE = 512
ATTN_TB = 1024
ATTN_SUB = 2
MOE_ROWS = 256
NEG_BIG = -1e30


def _dot(a, b):
    return jnp.dot(a, b, preferred_element_type=F32)


def _split_bf16(a):
    hi = a.astype(BF16)
    lo = (a - hi.astype(F32)).astype(BF16)
    return hi, lo


def _rms(x, g):
    return x * lax.rsqrt(jnp.mean(x * x, axis=-1, keepdims=True) + EPS) * g


def _params(semantics):
    return pltpu.CompilerParams(dimension_semantics=semantics, vmem_limit_bytes=VMEM_LIMIT)


def _const_spec(shape):
    nd = len(shape)
    return pl.BlockSpec(shape, lambda *_: (0,) * nd, pipeline_mode=pl.Buffered(1))


def _ada_body(c_ref, w_ref, b_ref, o_ref):
    ch, cl = _split_bf16(c_ref[...])
    wh, wl = _split_bf16(w_ref[0])
    o_ref[0] = _dot(ch, wh) + _dot(cl, wh) + _dot(ch, wl) + b_ref[0]


def _ada_mod(c, w_ada, b_ada):
    depth, d, n6 = w_ada.shape
    b = c.shape[0]
    tn = 1024
    c_pad = jnp.zeros((SUBLANES, d), F32).at[:b].set(c)
    out = pl.pallas_call(
        _ada_body,
        grid=(depth, n6 // tn),
        in_specs=[pl.BlockSpec((SUBLANES, d), lambda l, j: (0, 0)),
                  pl.BlockSpec((1, d, tn), lambda l, j: (l, 0, j)),
                  pl.BlockSpec((1, 1, tn), lambda l, j: (l, 0, j))],
        out_specs=pl.BlockSpec((1, SUBLANES, tn), lambda l, j: (l, 0, j)),
        out_shape=jax.ShapeDtypeStruct((depth, SUBLANES, n6), F32),
        compiler_params=_params(("arbitrary", "arbitrary")),
        name="ada_mod",
    )(c_pad, w_ada, b_ada.reshape(depth, 1, n6))
    return out[:, :b].reshape(depth, b, 6, d)


def _rope_body(pos_ref, invf_ref, cos_ref, sin_ref):
    ang = pos_ref[...] * invf_ref[...]
    cos_ref[...] = jnp.cos(ang)
    sin_ref[...] = jnp.sin(ang)


def _rope_tables(positions):
    n = positions.size
    half = QK_ROPE // 2
    per_row = LANES // half
    inv_freq = jnp.power(ROPE_THETA, -jnp.arange(0, QK_ROPE, 2, dtype=F32) / QK_ROPE)
    pos = jnp.repeat(positions.reshape(n // per_row, per_row).astype(F32), half, axis=1)
    invf = jnp.tile(inv_freq, per_row).reshape(1, LANES)
    rows = n // per_row
    tr = min(rows, 1024)
    cos, sin = pl.pallas_call(
        _rope_body,
        grid=(rows // tr,),
        in_specs=[pl.BlockSpec((tr, LANES), lambda i: (i, 0)), pl.BlockSpec((1, LANES), lambda i: (0, 0))],
        out_specs=[pl.BlockSpec((tr, LANES), lambda i: (i, 0))] * 2,
        out_shape=[jax.ShapeDtypeStruct((rows, LANES), F32)] * 2,
        compiler_params=_params(("arbitrary",)),
        name="rope_tables",
    )(pos, invf)
    z = jnp.zeros((n, LANES - QK_ROPE), F32)
    cos, sin = cos.reshape(n, half), sin.reshape(n, half)
    return jnp.concatenate([cos, cos, z], axis=1), jnp.concatenate([sin, sin, z], axis=1)


def _mix_in_body(x_ref, mod_ref, g1_ref, win_ref, poolw_ref, pscale_ref, gnp_ref, qg_ref, kvg_ref,
                 wuq_ref, wuqr_ref, wukv_ref, cs_ref, sn_ref,
                 ypool_ref, ussm_ref, q_ref, k_ref, v_ref, ext_ref, *, tiles_per_batch, tm, heads, scale):
    tin = pl.program_id(0) % tiles_per_batch
    mod = mod_ref[0]
    sh_a, sc_a = mod[0:1], mod[1:2]
    h = _rms(x_ref[...], g1_ref[...]) * (1.0 + sc_a) + sh_a
    z = _dot(h.astype(BF16), win_ref[...])

    halo = max(POOL_WINDOWS)
    pw = len(POOL_WINDOWS) * LANES

    @pl.when(tin == 0)
    def _():
        ext_ref[0:halo, :] = jnp.zeros((halo, pw), F32)

    zp = z[:, 0:pw]
    ext_ref[halo:halo + tm, :] = zp
    t = tin * tm + lax.broadcasted_iota(jnp.int32, (tm, 1), 0)
    ys = []
    for gi, w in enumerate(POOL_WINDOWS):
        cols = slice(gi * LANES, (gi + 1) * LANES)
        tok = zp[:, cols]
        s = tok
        for j in range(1, w):
            s = s + ext_ref[halo - j:halo - j + tm, cols]
        cnt = jnp.minimum(t + 1, w).astype(F32)
        ys.append(_dot((s / cnt - tok).astype(BF16), poolw_ref[gi]))
    ypool = jnp.concatenate(ys, axis=1) * pscale_ref[...]
    ypool_ref[...] = _rms(ypool, gnp_ref[...]).astype(BF16)
    ext_ref[0:halo, :] = ext_ref[tm:tm + halo, :]

    ussm_ref[...] = z[:, pw:2 * pw]

    cs, sn = cs_ref[...], sn_ref[...]
    qn = _rms(z[:, 1024:1536], qg_ref[...]).astype(BF16)
    qm = _dot(qn, wuq_ref[...])
    qr = _dot(qn, wuqr_ref[...])
    cs_q, sn_q = cs * scale, sn * scale
    for hh in range(heads):
        o = hh * 2 * LANES
        q_ref[hh, :, 0:LANES] = (qm[:, o:o + LANES] * scale).astype(BF16)
        q_ref[hh, :, LANES:2 * LANES] = (qm[:, o + LANES:o + 2 * LANES] * cs_q
                                         + qr[:, hh * LANES:(hh + 1) * LANES] * sn_q).astype(BF16)
    kvn = _rms(z[:, 1536:1792], kvg_ref[...]).astype(BF16)
    kv = _dot(kvn, wukv_ref[...])
    kpe = (z[:, 1792:1920] * cs + z[:, 1920:2048] * sn).astype(BF16)
    for hh in range(heads):
        k_ref[hh, :, 0:LANES] = kv[:, hh * LANES:(hh + 1) * LANES].astype(BF16)
        k_ref[hh, :, LANES:2 * LANES] = kpe
    v_ref[...] = kv[:, heads * LANES:].astype(BF16)


def _mix_in(x, mod, g1, win, poolw, pscale, gnp, qg, kvg, wuq, wuqr, wukv, cs, sn, *, seq, heads):
    n, d = x.shape
    tm = TOKEN_TILE
    tpb = seq // tm
    pw = len(POOL_WINDOWS) * LANES
    halo = max(POOL_WINDOWS)
    scale = float((QK_NOPE + QK_ROPE) ** -0.5 * math.log2(math.e))
    row = lambda w: pl.BlockSpec((tm, w), lambda i: (i, 0))
    hd = pl.BlockSpec((heads, tm, 2 * LANES), lambda i: (0, i, 0))
    return pl.pallas_call(
        functools.partial(_mix_in_body, tiles_per_batch=tpb, tm=tm, heads=heads, scale=scale),
        grid=(n // tm,),
        in_specs=[row(d),
                  pl.BlockSpec((1, 6, d), lambda i: (i // tpb, 0, 0)),
                  _const_spec(g1.shape), _const_spec(win.shape), _const_spec(poolw.shape),
                  _const_spec(pscale.shape), _const_spec(gnp.shape), _const_spec(qg.shape),
                  _const_spec(kvg.shape), _const_spec(wuq.shape), _const_spec(wuqr.shape),
                  _const_spec(wukv.shape), row(LANES), row(LANES)],
        out_specs=[row(pw), row(pw), hd, hd, row(heads * V_HEAD)],
        out_shape=[jax.ShapeDtypeStruct((n, pw), BF16), jax.ShapeDtypeStruct((n, pw), F32),
                   jax.ShapeDtypeStruct((heads, n, 2 * LANES), BF16),
                   jax.ShapeDtypeStruct((heads, n, 2 * LANES), BF16),
                   jax.ShapeDtypeStruct((n, heads * V_HEAD), BF16)],
        scratch_shapes=[pltpu.VMEM((tm + halo, pw), F32)],
        compiler_params=_params(("arbitrary",)),
        name="mix_in",
    )(x, mod, g1, win, poolw, pscale, gnp, qg, kvg, wuq, wuqr, wukv, cs, sn)


def _s5_body(u_ref, m_ref, bc_ref, cc_ref, tab_ref, y_ref, xcat_ref, sre_ref, sim_ref, car_ref, *, nc):
    t_ch = SSM_CHUNK
    sw = tab_ref.shape[-1]

    @pl.when(pl.program_id(2) == 0)
    def _():
        car_ref[...] = jnp.zeros(car_ref.shape, F32)

    for t in range(t_ch):
        xcat_ref[:, t * LANES:(t + 1) * LANES] = u_ref[pl.ds(t, nc, stride=t_ch), :].astype(BF16)
    xcat = xcat_ref[...]
    y_in = _dot(xcat, m_ref[0])
    x = _dot(xcat, bc_ref[0])
    sre_ref[...] = x[:, 0:sw]
    sim_ref[...] = x[:, sw:2 * sw]
    first_row = lax.broadcasted_iota(jnp.int32, (SUBLANES, sw), 0) == 0

    def block(b, carry):
        cre, cim = carry
        r0 = pl.multiple_of(b * SUBLANES, SUBLANES)
        re = sre_ref[pl.ds(r0, SUBLANES), :]
        im = sim_ref[pl.ds(r0, SUBLANES), :]
        for n, k in enumerate((1, 2, 4)):
            tr, ti = tab_ref[0, 2 * n], tab_ref[0, 2 * n + 1]
            pre, pim = pltpu.roll(re, k, 0), pltpu.roll(im, k, 0)
            re, im = re + tr * pre - ti * pim, im + tr * pim + ti * pre
        pr, pi = tab_ref[0, 6], tab_ref[0, 7]
        ore = re + pr * cre - pi * cim
        oim = im + pr * cim + pi * cre
        sre_ref[pl.ds(r0, SUBLANES), :] = jnp.where(first_row, cre, pltpu.roll(ore, 1, 0))
        sim_ref[pl.ds(r0, SUBLANES), :] = jnp.where(first_row, cim, pltpu.roll(oim, 1, 0))
        last = SUBLANES - 1
        return (jnp.broadcast_to(ore[last:last + 1, :], (SUBLANES, sw)),
                jnp.broadcast_to(oim[last:last + 1, :], (SUBLANES, sw)))

    cre, cim = lax.fori_loop(0, nc // SUBLANES, block, (car_ref[0], car_ref[1]))
    car_ref[0] = cre
    car_ref[1] = cim
    sp = jnp.concatenate([sre_ref[...], sim_ref[...]], axis=1).astype(BF16)
    y = y_in + _dot(sp, cc_ref[0])
    for t in range(t_ch):
        y_ref[pl.ds(t, nc, stride=t_ch), :] = y[:, t * LANES:(t + 1) * LANES]


def _s5_scan(u, m, bc, cc, tab, *, batch):
    n, width = u.shape
    tiles = width // LANES
    seq = n // batch
    rs = min(S5_ROWS, seq)
    steps = seq // rs
    nc = rs // SSM_CHUNK
    sw = tab.shape[-1]
    kw = SSM_CHUNK * LANES
    blk = pl.BlockSpec((rs, LANES), lambda j, b, r: (b * steps + r, j))
    return pl.pallas_call(
        functools.partial(_s5_body, nc=nc),
        grid=(tiles, batch, steps),
        in_specs=[blk,
                  pl.BlockSpec((1, kw, kw), lambda j, b, r: (j, 0, 0)),
                  pl.BlockSpec((1, kw, 2 * sw), lambda j, b, r: (j, 0, 0)),
                  pl.BlockSpec((1, 2 * sw, kw), lambda j, b, r: (j, 0, 0)),
                  pl.BlockSpec((1, 8, SUBLANES, sw), lambda j, b, r: (j, 0, 0, 0))],
        out_specs=blk,
        out_shape=jax.ShapeDtypeStruct((n, width), F32),
        scratch_shapes=[pltpu.VMEM((nc, kw), BF16), pltpu.VMEM((nc, sw), F32), pltpu.VMEM((nc, sw), F32),
                        pltpu.VMEM((2, SUBLANES, sw), F32)],
        compiler_params=_params(("arbitrary", "arbitrary", "arbitrary")),
        name="s5_scan",
    )(u, m, bc, cc, tab)


def _ssm_tables(lam_re, lam_im, log_dt, b_re, b_im, c_re, c_im):
    hp = lax.Precision.HIGHEST
    g, p = lam_re.shape
    hh = b_re.shape[-1]
    t = SSM_CHUNK
    gq = LANES // hh
    tiles = g // gq
    dt = jnp.exp(log_dt)[:, None]
    mag = jnp.exp(lam_re * dt)
    ar, ai = mag * jnp.cos(lam_im * dt), mag * jnp.sin(lam_im * dt)
    den = lam_re * lam_re + lam_im * lam_im
    fr = ((ar - 1.0) * lam_re + ai * lam_im) / den
    fi = (ai * lam_re - (ar - 1.0) * lam_im) / den
    bbr = fr[..., None] * b_re - fi[..., None] * b_im
    bbi = fr[..., None] * b_im + fi[..., None] * b_re

    def powers(xr, xi, count):
        prs, pis = [jnp.ones_like(xr)], [jnp.zeros_like(xi)]
        for _ in range(count):
            prs.append(prs[-1] * xr - pis[-1] * xi)
            pis.append(prs[-2] * xi + pis[-1] * xr)
        return jnp.stack(prs), jnp.stack(pis)

    pr, pi = powers(ar, ai, t)
    car = c_re[None] * pr[:, :, None, :] - c_im[None] * pi[:, :, None, :]
    cai = c_im[None] * pr[:, :, None, :] + c_re[None] * pi[:, :, None, :]
    kern = (jnp.einsum('kgop,gpi->gkio', car[:t], bbr, precision=hp)
            - jnp.einsum('kgop,gpi->gkio', cai[:t], bbi, precision=hp))
    eye = jnp.eye(gq, dtype=F32)
    kw = t * gq * hh
    kq = kern.reshape(tiles, gq, t, hh, hh).transpose(0, 2, 1, 3, 4)
    lag_blk = (kq[:, :, :, :, None, :] * eye[None, None, :, None, :, None]).reshape(tiles, t, LANES, LANES)
    lag = jnp.arange(t)[None, :] - jnp.arange(t)[:, None]
    toep = jnp.where((lag >= 0)[None, :, :, None, None], lag_blk.astype(BF16)[:, jnp.clip(lag, 0)], 0)
    m = toep.transpose(0, 1, 3, 2, 4).reshape(tiles, kw, kw)
    rev = jnp.arange(t - 1, -1, -1)
    bcr = (pr[rev][..., None] * bbr[None] - pi[rev][..., None] * bbi[None]).transpose(1, 0, 3, 2)
    bci = (pr[rev][..., None] * bbi[None] + pi[rev][..., None] * bbr[None]).transpose(1, 0, 3, 2)
    bri = jnp.stack([bcr, bci], axis=3).reshape(tiles, gq, t, hh, 2, p)
    bc = (bri.transpose(0, 2, 1, 3, 4, 5)[:, :, :, :, :, None, :]
          * eye[None, None, :, None, None, :, None]).reshape(tiles, kw, 2 * gq * p)
    ccr = car[1:].transpose(1, 3, 0, 2)
    cci = -cai[1:].transpose(1, 3, 0, 2)
    cri = jnp.stack([ccr, cci], axis=1).reshape(tiles, gq, 2, p, t, hh)
    cc = (cri.transpose(0, 2, 1, 3, 4, 5)[:, :, :, :, :, None, :]
          * eye[None, None, :, None, None, :, None]).reshape(tiles, 2 * gq * p, kw)
    a16r, a16i = pr[t].reshape(tiles, gq * p), pi[t].reshape(tiles, gq * p)
    qr, qi = powers(a16r, a16i, SUBLANES)
    sub = jnp.arange(SUBLANES)
    tabs = []
    for k in (1, 2, 4):
        keep = (sub >= k).astype(F32)[None, :, None]
        tabs += [qr[k][:, None, :] * keep, qi[k][:, None, :] * keep]
    tabs += [qr[1:].transpose(1, 0, 2), qi[1:].transpose(1, 0, 2)]
    tab = jnp.stack(tabs, axis=1)
    return m.astype(BF16), bc.astype(BF16), cc.astype(BF16), tab


def _attn_body(q_ref, k_ref, v_ref, o_ref, *, tb, nsub):
    i = pl.program_id(1)
    qs = [q_ref[0, s * tb:(s + 1) * tb, :] for s in range(nsub)]
    diag = lax.broadcasted_iota(jnp.int32, (tb, tb), 1) <= lax.broadcasted_iota(jnp.int32, (tb, tb), 0)

    def load(j):
        k0 = pl.multiple_of(j * tb, tb)
        return k_ref[0, pl.ds(k0, tb), :], v_ref[pl.ds(k0, tb), :]

    def step(q, kb, vb, carry, masked):
        m, l, acc = carry
        s = lax.dot_general(q, kb, (((1,), (1,)), ((), ())), preferred_element_type=F32)
        if masked:
            s = jnp.where(diag, s, NEG_BIG)
        m_new = jnp.maximum(m, jnp.max(s, axis=-1, keepdims=True))
        p = jnp.exp2(s - m_new)
        alpha = jnp.exp2(m - m_new)
        l = alpha * l + jnp.sum(p, axis=-1, keepdims=True)
        acc = alpha * acc + _dot(p.astype(BF16), vb)
        return m_new, l, acc

    def body(j, carries):
        kb, vb = load(j)
        return tuple(step(qs[s], kb, vb, carries[s], False) for s in range(nsub))

    init = (jnp.full((tb, 1), NEG_BIG, F32), jnp.zeros((tb, 1), F32), jnp.zeros((tb, V_HEAD), F32))
    n_full = i * nsub
    carries = list(lax.fori_loop(0, n_full, body, (init,) * nsub))
    for d in range(nsub):
        kb, vb = load(n_full + d)
        for s in range(d, nsub):
            carries[s] = step(qs[s], kb, vb, carries[s], s == d)
    for s in range(nsub):
        _, l, acc = carries[s]
        o_ref[s * tb:(s + 1) * tb, :] = (acc / l).astype(BF16)


def _mla_attn(q, k, v, *, batch, seq, heads):
    tb, nsub = ATTN_TB, ATTN_SUB
    tq = tb * nsub
    nq = seq // tq
    n = batch * seq
    return pl.pallas_call(
        functools.partial(_attn_body, tb=tb, nsub=nsub),
        grid=(batch * heads, nq),
        in_specs=[pl.BlockSpec((1, tq, 2 * LANES), lambda bh, i: (bh % heads, (bh // heads) * nq + i, 0)),
                  pl.BlockSpec((1, seq, 2 * LANES), lambda bh, i: (bh % heads, bh // heads, 0)),
                  pl.BlockSpec((seq, V_HEAD), lambda bh, i: (bh // heads, bh % heads))],
        out_specs=pl.BlockSpec((tq, V_HEAD), lambda bh, i: ((bh // heads) * nq + i, bh % heads)),
        out_shape=jax.ShapeDtypeStruct((n, heads * V_HEAD), BF16),
        compiler_params=_params(("arbitrary", "arbitrary")),
        name="mla_attn",
    )(q, k, v)


def _gelu_tanh(x):
    return 0.5 * x * (1.0 + jnp.tanh(math.sqrt(2.0 / math.pi) * (x + 0.044715 * (x * x * x))))


def _mix_out_body(x_ref, yp_ref, ysc_ref, us_ref, ym_ref, mod_ref, dsk_ref, wglu_ref, gns_ref, gnm_ref,
                  wo_ref, g2_ref, wrh_ref, wrl_ref, br_ref,
                  x1_ref, h2_ref, ridx_ref, rw_ref, *, n_exp, n_grp):
    mod = mod_ref[0]
    g_a, sh_f, sc_f = mod[2:3], mod[3:4], mod[4:5]
    ys = _gelu_tanh(ysc_ref[...] + dsk_ref[...] * us_ref[...])
    ys = ys * jax.nn.sigmoid(_dot(ys.astype(BF16), wglu_ref[...]))
    ysn = _rms(ys, gns_ref[...]).astype(BF16)
    ymn = _rms(ym_ref[...].astype(F32), gnm_ref[...]).astype(BF16)
    wp, ws = yp_ref.shape[1], ysc_ref.shape[1]
    o = (_dot(yp_ref[...], wo_ref[0:wp, :]) + _dot(ysn, wo_ref[wp:wp + ws, :])
         + _dot(ymn, wo_ref[wp + ws:, :]))
    x1 = x_ref[...] + g_a * o
    x1_ref[...] = x1
    h2 = _rms(x1, g2_ref[...]) * (1.0 + sc_f) + sh_f
    h2_ref[...] = h2

    hh, hl = _split_bf16(h2)
    logits = _dot(hh, wrh_ref[...]) + _dot(hl, wrh_ref[...]) + _dot(hh, wrl_ref[...]) + br_ref[...]
    lane = lax.broadcasted_iota(jnp.int32, logits.shape, 1).astype(F32)
    far = float(2 * LANES)
    per = n_exp // n_grp
    gl = jnp.where(lane >= n_exp, jnp.where(lane < n_exp + n_grp, logits, NEG_BIG), NEG_BIG)
    gmax = jnp.max(gl, axis=-1, keepdims=True)
    gtop = jnp.min(jnp.where(gl == gmax, lane, far), axis=-1, keepdims=True) - n_exp
    g_w = 1.0 / jnp.sum(jnp.exp(gl - gmax), axis=-1, keepdims=True)
    lo = gtop * per
    el = jnp.where(lane >= lo, jnp.where(lane < lo + per, logits, NEG_BIG), NEG_BIG)
    m1 = jnp.max(el, axis=-1, keepdims=True)
    i1 = jnp.min(jnp.where(el == m1, lane, far), axis=-1, keepdims=True)
    el2 = jnp.where(lane == i1, NEG_BIG, el)
    m2 = jnp.max(el2, axis=-1, keepdims=True)
    i2 = jnp.min(jnp.where(el2 == m2, lane, far), axis=-1, keepdims=True)
    e21 = jnp.exp(m2 - m1)
    w1 = g_w / (1.0 + e21)
    w2 = g_w * e21 / (1.0 + e21)
    ridx_ref[...] = jnp.where(lane == 0.0, i1, jnp.where(lane == 1.0, i2, 0.0)).astype(jnp.int32)
    rw_ref[...] = jnp.where(lane == 0.0, w1, jnp.where(lane == 1.0, w2, 0.0))


def _mix_out(x, ypool, yscan, ussm, ymla, mod, dsk, wglu, gns, gnm, wo, g2, wrh, wrl, br, *, seq, n_exp, n_grp):
    n, d = x.shape
    tm = TOKEN_TILE
    tpb = seq // tm
    row = lambda w: pl.BlockSpec((tm, w), lambda i: (i, 0))
    return pl.pallas_call(
        functools.partial(_mix_out_body, n_exp=n_exp, n_grp=n_grp),
        grid=(n // tm,),
        in_specs=[row(d), row(ypool.shape[1]), row(yscan.shape[1]), row(ussm.shape[1]), row(ymla.shape[1]),
                  pl.BlockSpec((1, 6, d), lambda i: (i // tpb, 0, 0)),
                  _const_spec(dsk.shape), _const_spec(wglu.shape), _const_spec(gns.shape), _const_spec(gnm.shape),
                  _const_spec(wo.shape), _const_spec(g2.shape), _const_spec(wrh.shape), _const_spec(wrl.shape),
                  _const_spec(br.shape)],
        out_specs=[row(d), row(d), row(LANES), row(LANES)],
        out_shape=[jax.ShapeDtypeStruct((n, d), F32), jax.ShapeDtypeStruct((n, d), F32),
                   jax.ShapeDtypeStruct((n, LANES), jnp.int32), jax.ShapeDtypeStruct((n, LANES), F32)],
        compiler_params=_params(("arbitrary",)),
        name="mix_out",
    )(x, ypool, yscan, ussm, ymla, mod, dsk, wglu, gns, gnm, wo, g2, wrh, wrl, br)


def _route_plan(ridx, n_exp, blk):
    expert = ridx[:, :TOP_K].reshape(-1)
    a = expert.shape[0]
    onehot = (expert[:, None] == jnp.arange(n_exp, dtype=jnp.int32)[None, :]).astype(jnp.int32)
    seg = 256
    oh3 = onehot.astype(F32).reshape(a // seg, seg, n_exp)
    intra = jnp.einsum('ij,bje->bie', jnp.tril(jnp.ones((seg, seg), F32)), oh3)
    tot = intra[:, -1, :]
    csum = (intra + (jnp.cumsum(tot, axis=0) - tot)[:, None, :]).reshape(a, n_exp).astype(jnp.int32)
    counts = csum[-1]
    padded = (counts + blk - 1) // blk * blk
    pad_end = jnp.cumsum(padded)
    pad_start = pad_end - padded
    dest = jnp.sum(onehot * (csum - 1 + pad_start[None, :]), axis=1).astype(jnp.int32)
    nb = a // blk + n_exp
    bstart = jnp.arange(nb, dtype=jnp.int32) * blk
    be = jnp.minimum(jnp.sum((pad_end[None, :] <= bstart[:, None]).astype(jnp.int32), axis=1), n_exp - 1)
    nv = jnp.clip(counts[be] - (bstart - pad_start[be]), 0, blk).astype(jnp.int32)
    return dest, be, nv, nb


def _dispatch_body(nv_ref, dest_ref, h_ref, xs_hbm, zero_ref, sem, zsem, *, tm, blk, nb):
    def zero_copy(j):
        return pltpu.make_async_copy(zero_ref, xs_hbm.at[pl.ds(j * blk, blk)], zsem)

    @pl.when(pl.program_id(0) == 0)
    def _():
        zero_ref[...] = jnp.zeros(zero_ref.shape, zero_ref.dtype)

        def fill(j, carry):
            @pl.when(nv_ref[j] < blk)
            def _():
                zero_copy(j).start()
            return carry

        def drain(j, carry):
            @pl.when(nv_ref[j] < blk)
            def _():
                zero_copy(j).wait()
            return carry

        lax.fori_loop(0, nb, fill, 0)
        lax.fori_loop(0, nb, drain, 0)

    def issue(r, carry):
        for k in range(TOP_K):
            pltpu.make_async_copy(h_ref.at[pl.ds(r, 1)], xs_hbm.at[pl.ds(dest_ref[TOP_K * r + k], 1)], sem).start()
        return carry

    lax.fori_loop(0, tm, issue, 0)
    for k in range(TOP_K):
        pltpu.make_async_copy(h_ref, xs_hbm.at[pl.ds(0, tm)], sem).wait()


def _dispatch(nv, dest, h2, nb):
    n, d = h2.shape
    tm = TOKEN_TILE
    blk = MOE_ROWS
    grid_spec = pltpu.PrefetchScalarGridSpec(
        num_scalar_prefetch=1,
        grid=(n // tm,),
        in_specs=[pl.BlockSpec((TOP_K * tm,), lambda i, nv: (i,), memory_space=pltpu.SMEM),
                  pl.BlockSpec((tm, d), lambda i, nv: (i, 0))],
        out_specs=pl.BlockSpec(memory_space=pl.ANY),
        scratch_shapes=[pltpu.VMEM((blk, d), h2.dtype), pltpu.SemaphoreType.DMA(()), pltpu.SemaphoreType.DMA(())],
    )
    return pl.pallas_call(
        functools.partial(_dispatch_body, tm=tm, blk=blk, nb=nb),
        grid_spec=grid_spec,
        out_shape=jax.ShapeDtypeStruct((nb * blk, d), h2.dtype),
        compiler_params=pltpu.CompilerParams(dimension_semantics=("arbitrary",), vmem_limit_bytes=VMEM_LIMIT,
                                             has_side_effects=True),
        name="moe_dispatch",
    )(nv, dest, h2)


def _expert_body(be_ref, nv_ref, xs_ref, w1_ref, w3_ref, w2_ref, ys_ref, w1b, w3b, w2b, *, blk):
    j = pl.program_id(0)
    e = be_ref[j]
    prev = be_ref[jnp.maximum(j - 1, 0)]

    @pl.when(jnp.logical_or(j == 0, e != prev))
    def _():
        w1b[...] = w1_ref[0, 0].astype(BF16)
        w3b[...] = w3_ref[0, 0].astype(BF16)
        w2b[...] = w2_ref[0, 0].astype(BF16)

    nv = nv_ref[j]

    @pl.when(nv > 0)
    def _():
        x = xs_ref[...].astype(BF16)
        a = _dot(x, w1b[...])
        b = _dot(x, w3b[...])
        ys_ref[...] = _dot((a * jax.nn.sigmoid(a) * b).astype(BF16), w2b[...])

    @pl.when(nv == 0)
    def _():
        ys_ref[...] = jnp.zeros(ys_ref.shape, F32)


def _experts(be, nv, xs, w1, w3, w2, nb, layer):
    blk = MOE_ROWS
    _, _, d, de = w1.shape
    grid_spec = pltpu.PrefetchScalarGridSpec(
        num_scalar_prefetch=2,
        grid=(nb,),
        in_specs=[pl.BlockSpec((blk, d), lambda j, be, nv: (j, 0)),
                  pl.BlockSpec((1, 1, d, de), lambda j, be, nv: (layer, be[j], 0, 0)),
                  pl.BlockSpec((1, 1, d, de), lambda j, be, nv: (layer, be[j], 0, 0)),
                  pl.BlockSpec((1, 1, de, d), lambda j, be, nv: (layer, be[j], 0, 0))],
        out_specs=pl.BlockSpec((blk, d), lambda j, be, nv: (j, 0)),
        scratch_shapes=[pltpu.VMEM((d, de), BF16), pltpu.VMEM((d, de), BF16), pltpu.VMEM((de, d), BF16)],
    )
    return pl.pallas_call(
        functools.partial(_expert_body, blk=blk),
        grid_spec=grid_spec,
        out_shape=jax.ShapeDtypeStruct((nb * blk, d), F32),
        compiler_params=_params(("arbitrary",)),
        name="moe_experts",
    )(be, nv, xs, w1, w3, w2)


def _combine_body(dest_ref, x1_ref, rw_ref, mod_ref, fg_ref, ys_hbm, out_ref, buf, sem, *, tm, final):
    def issue(r, carry):
        for k in range(TOP_K):
            pltpu.make_async_copy(ys_hbm.at[pl.ds(dest_ref[TOP_K * r + k], 1)], buf.at[k, pl.ds(r, 1)], sem).start()
        return carry

    lax.fori_loop(0, tm, issue, 0)
    for k in range(TOP_K):
        pltpu.make_async_copy(ys_hbm.at[pl.ds(0, tm)], buf.at[k], sem).wait()
    w = rw_ref[...]
    y = buf[0] * w[:, 0:1] + buf[1] * w[:, 1:2]
    x2 = x1_ref[...] + mod_ref[0][5:6] * y
    out_ref[...] = _rms(x2, fg_ref[...]) if final else x2


def _combine(dest, x1, rw, mod, fg, ys, *, seq, final):
    n, d = x1.shape
    tm = TOKEN_TILE
    tpb = seq // tm
    row = lambda w: pl.BlockSpec((tm, w), lambda i: (i, 0))
    return pl.pallas_call(
        functools.partial(_combine_body, tm=tm, final=final),
        grid=(n // tm,),
        in_specs=[pl.BlockSpec((TOP_K * tm,), lambda i: (i,), memory_space=pltpu.SMEM),
                  row(d), row(LANES),
                  pl.BlockSpec((1, 6, d), lambda i: (i // tpb, 0, 0)),
                  pl.BlockSpec((1, d), lambda i: (0, 0)),
                  pl.BlockSpec(memory_space=pl.ANY)],
        out_specs=row(d),
        out_shape=jax.ShapeDtypeStruct((n, d), F32),
        scratch_shapes=[pltpu.VMEM((TOP_K, tm, d), F32), pltpu.SemaphoreType.DMA(())],
        compiler_params=_params(("arbitrary",)),
        name="moe_combine",
    )(dest, x1, rw, mod, fg, ys)


def _rot_half_cols(w):
    half = w.shape[-1] // 2
    return jnp.concatenate([-w[..., half:], w[..., :half]], axis=-1)


def _layer_weights(l, w_in, pool_w, w_uq, w_ukv, w_out, ssm_w_glu, router_w_group, router_b_group,
                   router_w_expert, router_b_expert):
    d = w_in.shape[1]
    heads = w_uq.shape[2]
    wi = w_in[l]
    kpe = wi[:, -QK_ROPE:]
    pad = jnp.zeros((d, LANES - QK_ROPE), F32)
    win = jnp.concatenate([wi[:, :-QK_ROPE], kpe, pad, _rot_half_cols(kpe), pad], axis=1).astype(BF16)
    uq = w_uq[l]
    r = uq.shape[0]
    zq = jnp.zeros((r, heads, LANES - QK_ROPE), F32)
    wuq = jnp.concatenate([uq, zq], axis=-1).reshape(r, heads * 2 * LANES).astype(BF16)
    wuqr = jnp.concatenate([_rot_half_cols(uq[..., QK_NOPE:]), zq], axis=-1).reshape(r, heads * LANES).astype(BF16)
    ukv = w_ukv[l]
    wukv = jnp.concatenate([ukv[..., :QK_NOPE].reshape(ukv.shape[0], -1),
                            ukv[..., QK_NOPE:].reshape(ukv.shape[0], -1)], axis=1).astype(BF16)
    n_grp = router_w_group.shape[-1]
    n_exp = router_w_expert.shape[-1]
    wr = jnp.concatenate([router_w_expert[l], router_w_group[l],
                          jnp.zeros((d, LANES - n_exp - n_grp), F32)], axis=1)
    wrh, wrl = _split_bf16(wr)
    br = jnp.concatenate([router_b_expert[l], router_b_group[l],
                          jnp.zeros((LANES - n_exp - n_grp,), F32)]).reshape(1, LANES)
    return dict(win=win, poolw=pool_w[l].astype(BF16), wuq=wuq, wuqr=wuqr, wukv=wukv,
                wo=w_out[l].astype(BF16), wglu=ssm_w_glu[l].astype(BF16), wrh=wrh, wrl=wrl, br=br)


def kernel(x, c, positions, w_ada, b_ada, norm1_g, w_in, pool_w, pool_scale, ssm_lam_re, ssm_lam_im, ssm_log_dt, ssm_b_re, ssm_b_im, ssm_c_re, ssm_c_im, ssm_d, ssm_w_glu, q_norm_g, kv_norm_g, w_uq, w_ukv, out_norm_g, w_out, norm2_g, router_w_group, router_b_group, router_w_expert, router_b_expert, w_gate, w_up, w_down, final_g):
    batch, seq, d = x.shape
    depth = w_ada.shape[0]
    n = batch * seq
    heads = w_uq.shape[2]
    n_grp = router_w_group.shape[-1]
    n_exp = router_w_expert.shape[-1]
    pw = pool_w.shape[1] * pool_w.shape[2]
    sw = ssm_d.shape[-1]
    assert SSM_CHUNK * SSM_GROUP == MXU_DIM and pw == len(POOL_WINDOWS) * LANES and sw % LANES == 0
    assert seq % (ATTN_TB * ATTN_SUB) == 0 and seq % TOKEN_TILE == 0 and (n * TOP_K) % MOE_ROWS == 0

    mod_all = _ada_mod(c, w_ada, b_ada)
    cs, sn = _rope_tables(positions)
    xf = x.reshape(n, d)
    for l in range(depth):
        wts = _layer_weights(l, w_in, pool_w, w_uq, w_ukv, w_out, ssm_w_glu, router_w_group, router_b_group,
                             router_w_expert, router_b_expert)
        mod = mod_all[l]
        gn = out_norm_g[l]
        ypool, ussm, q, k, v = _mix_in(
            xf, mod, norm1_g[l].reshape(1, d), wts['win'], wts['poolw'], pool_scale[l].reshape(1, pw),
            gn[:pw].reshape(1, pw), q_norm_g[l].reshape(1, -1), kv_norm_g[l].reshape(1, -1),
            wts['wuq'], wts['wuqr'], wts['wukv'], cs, sn, seq=seq, heads=heads)
        m, bc, cc, tab = _ssm_tables(ssm_lam_re[l], ssm_lam_im[l], ssm_log_dt[l], ssm_b_re[l], ssm_b_im[l],
                                     ssm_c_re[l], ssm_c_im[l])
        yscan = _s5_scan(ussm, m, bc, cc, tab, batch=batch)
        ymla = _mla_attn(q, k, v, batch=batch, seq=seq, heads=heads)
        x1, h2, ridx, rw = _mix_out(
            xf, ypool, yscan, ussm, ymla, mod, ssm_d[l].reshape(1, sw), wts['wglu'],
            gn[pw:pw + sw].reshape(1, sw), gn[pw + sw:].reshape(1, -1), wts['wo'], norm2_g[l].reshape(1, d),
            wts['wrh'], wts['wrl'], wts['br'], seq=seq, n_exp=n_exp, n_grp=n_grp)
        dest, be, nv, nb = _route_plan(ridx, n_exp, MOE_ROWS)
        xs = _dispatch(nv, dest, h2, nb)
        ys = _experts(be, nv, xs, w_gate, w_up, w_down, nb, l)
        xf = _combine(dest, x1, rw, mod, final_g.reshape(1, d), ys, seq=seq, final=(l == depth - 1))
    return xf.reshape(batch, seq, d)
```

```python
import functools
import math

import jax
import jax.numpy as jnp
from jax import lax
from jax.experimental import pallas as pl
from jax.experimental.pallas import tpu as pltpu

F32 = jnp.float32
BF16 = jnp.bfloat16

POOL_WINDOWS = (2, 4, 8, 16)
SSM_GROUP = 16
QK_NOPE = 128
QK_ROPE = 64
V_HEAD = 128
ROPE_THETA = 10000.0
EPS = 1e-6
TOP_K = 2

LANES = 128
SUBLANES = 8
MXU_DIM = 256
VMEM_LIMIT = 56 * 1024 * 1024

SSM_CHUNK = 16
S5_ROWS = 4096
TOKEN_TILE = 512
ATTN_TB = 1024
ATTN_SUB = 2
MOE_ROWS = 256
ROW_DMA_UNROLL = 4
NEG_BIG = -1e30


def _dot(a, b):
    return jnp.dot(a, b, preferred_element_type=F32)


def _split_bf16(a):
    hi = a.astype(BF16)
    lo = (a - hi.astype(F32)).astype(BF16)
    return hi, lo


def _rms(x, g):
    return x * lax.rsqrt(jnp.mean(x * x, axis=-1, keepdims=True) + EPS) * g


def _params(semantics):
    return pltpu.CompilerParams(dimension_semantics=semantics, vmem_limit_bytes=VMEM_LIMIT)


def _const_spec(shape):
    nd = len(shape)
    return pl.BlockSpec(shape, lambda *_: (0,) * nd, pipeline_mode=pl.Buffered(1))


def _ada_body(c_ref, w_ref, b_ref, o_ref):
    ch, cl = _split_bf16(c_ref[...])
    wh, wl = _split_bf16(w_ref[0])
    o_ref[0] = _dot(ch, wh) + _dot(cl, wh) + _dot(ch, wl) + b_ref[0]


def _ada_mod(c, w_ada, b_ada):
    depth, d, n6 = w_ada.shape
    b = c.shape[0]
    tn = 1024
    c_pad = jnp.zeros((SUBLANES, d), F32).at[:b].set(c)
    out = pl.pallas_call(
        _ada_body,
        grid=(depth, n6 // tn),
        in_specs=[pl.BlockSpec((SUBLANES, d), lambda l, j: (0, 0)),
                  pl.BlockSpec((1, d, tn), lambda l, j: (l, 0, j)),
                  pl.BlockSpec((1, 1, tn), lambda l, j: (l, 0, j))],
        out_specs=pl.BlockSpec((1, SUBLANES, tn), lambda l, j: (l, 0, j)),
        out_shape=jax.ShapeDtypeStruct((depth, SUBLANES, n6), F32),
        compiler_params=_params(("arbitrary", "arbitrary")),
        name="ada_mod",
    )(c_pad, w_ada, b_ada.reshape(depth, 1, n6))
    return out[:, :b].reshape(depth, b, 6, d)


def _rope_body(pos_ref, invf_ref, cos_ref, sin_ref):
    ang = pos_ref[...] * invf_ref[...]
    cos_ref[...] = jnp.cos(ang)
    sin_ref[...] = jnp.sin(ang)


def _rope_tables(positions):
    n = positions.size
    half = QK_ROPE // 2
    per_row = LANES // half
    inv_freq = jnp.power(ROPE_THETA, -jnp.arange(0, QK_ROPE, 2, dtype=F32) / QK_ROPE)
    pos = jnp.repeat(positions.reshape(n // per_row, per_row).astype(F32), half, axis=1)
    invf = jnp.tile(inv_freq, per_row).reshape(1, LANES)
    rows = n // per_row
    tr = min(rows, 1024)
    cos, sin = pl.pallas_call(
        _rope_body,
        grid=(rows // tr,),
        in_specs=[pl.BlockSpec((tr, LANES), lambda i: (i, 0)), pl.BlockSpec((1, LANES), lambda i: (0, 0))],
        out_specs=[pl.BlockSpec((tr, LANES), lambda i: (i, 0))] * 2,
        out_shape=[jax.ShapeDtypeStruct((rows, LANES), F32)] * 2,
        compiler_params=_params(("arbitrary",)),
        name="rope_tables",
    )(pos, invf)
    z = jnp.zeros((n, LANES - QK_ROPE), F32)
    cos, sin = cos.reshape(n, half), sin.reshape(n, half)
    return jnp.concatenate([cos, cos, z], axis=1), jnp.concatenate([sin, sin, z], axis=1)


def _mix_in_body(x_ref, mod_ref, g1_ref, win_ref, poolw_ref, pscale_ref, gnp_ref, qg_ref, kvg_ref,
                 wuq_ref, wuqr_ref, wukv_ref, cs_ref, sn_ref,
                 ypool_ref, ussm_ref, q_ref, k_ref, v_ref, ext_ref, *, tiles_per_batch, tm, heads, scale):
    tin = pl.program_id(0) % tiles_per_batch
    mod = mod_ref[0]
    sh_a, sc_a = mod[0:1], mod[1:2]
    h = _rms(x_ref[...], g1_ref[...]) * (1.0 + sc_a) + sh_a
    z = _dot(h.astype(BF16), win_ref[...])

    halo = max(POOL_WINDOWS)
    pw = len(POOL_WINDOWS) * LANES

    @pl.when(tin == 0)
    def _():
        ext_ref[0:halo, :] = jnp.zeros((halo, pw), F32)

    zp = z[:, 0:pw]
    ext_ref[halo:halo + tm, :] = zp
    t = tin * tm + lax.broadcasted_iota(jnp.int32, (tm, 1), 0)
    ys = []
    for gi, w in enumerate(POOL_WINDOWS):
        cols = slice(gi * LANES, (gi + 1) * LANES)
        tok = zp[:, cols]
        s = tok
        for j in range(1, w):
            s = s + ext_ref[halo - j:halo - j + tm, cols]
        cnt = jnp.minimum(t + 1, w).astype(F32)
        ys.append(_dot((s / cnt - tok).astype(BF16), poolw_ref[gi]))
    ypool = jnp.concatenate(ys, axis=1) * pscale_ref[...]
    ypool_ref[...] = _rms(ypool, gnp_ref[...]).astype(BF16)
    ext_ref[0:halo, :] = ext_ref[tm:tm + halo, :]

    ussm_ref[...] = z[:, pw:2 * pw]

    cs, sn = cs_ref[...], sn_ref[...]
    qn = _rms(z[:, 1024:1536], qg_ref[...]).astype(BF16)
    qm = _dot(qn, wuq_ref[...])
    qr = _dot(qn, wuqr_ref[...])
    cs_q, sn_q = cs * scale, sn * scale
    for hh in range(heads):
        o = hh * 2 * LANES
        q_ref[hh, :, 0:LANES] = (qm[:, o:o + LANES] * scale).astype(BF16)
        q_ref[hh, :, LANES:2 * LANES] = (qm[:, o + LANES:o + 2 * LANES] * cs_q
                                         + qr[:, hh * LANES:(hh + 1) * LANES] * sn_q).astype(BF16)
    kvn = _rms(z[:, 1536:1792], kvg_ref[...]).astype(BF16)
    kv = _dot(kvn, wukv_ref[...])
    kpe = (z[:, 1792:1920] * cs + z[:, 1920:2048] * sn).astype(BF16)
    for hh in range(heads):
        k_ref[hh, :, 0:LANES] = kv[:, hh * LANES:(hh + 1) * LANES].astype(BF16)
        k_ref[hh, :, LANES:2 * LANES] = kpe
    v_ref[...] = kv[:, heads * LANES:].astype(BF16)


def _mix_in(x, mod, g1, win, poolw, pscale, gnp, qg, kvg, wuq, wuqr, wukv, cs, sn, *, seq, heads):
    n, d = x.shape
    tm = TOKEN_TILE
    tpb = seq // tm
    pw = len(POOL_WINDOWS) * LANES
    halo = max(POOL_WINDOWS)
    scale = float((QK_NOPE + QK_ROPE) ** -0.5 * math.log2(math.e))
    row = lambda w: pl.BlockSpec((tm, w), lambda i: (i, 0))
    hd = pl.BlockSpec((heads, tm, 2 * LANES), lambda i: (0, i, 0))
    return pl.pallas_call(
        functools.partial(_mix_in_body, tiles_per_batch=tpb, tm=tm, heads=heads, scale=scale),
        grid=(n // tm,),
        in_specs=[row(d),
                  pl.BlockSpec((1, 6, d), lambda i: (i // tpb, 0, 0)),
                  _const_spec(g1.shape), _const_spec(win.shape), _const_spec(poolw.shape),
                  _const_spec(pscale.shape), _const_spec(gnp.shape), _const_spec(qg.shape),
                  _const_spec(kvg.shape), _const_spec(wuq.shape), _const_spec(wuqr.shape),
                  _const_spec(wukv.shape), row(LANES), row(LANES)],
        out_specs=[row(pw), row(pw), hd, hd, row(heads * V_HEAD)],
        out_shape=[jax.ShapeDtypeStruct((n, pw), BF16), jax.ShapeDtypeStruct((n, pw), F32),
                   jax.ShapeDtypeStruct((heads, n, 2 * LANES), BF16),
                   jax.ShapeDtypeStruct((heads, n, 2 * LANES), BF16),
                   jax.ShapeDtypeStruct((n, heads * V_HEAD), BF16)],
        scratch_shapes=[pltpu.VMEM((tm + halo, pw), F32)],
        compiler_params=_params(("arbitrary",)),
        name="mix_in",
    )(x, mod, g1, win, poolw, pscale, gnp, qg, kvg, wuq, wuqr, wukv, cs, sn)


def _s5_body(u_ref, m_ref, bc_ref, cc_ref, tab_ref, y_ref, xcat_ref, sre_ref, sim_ref, car_ref, *, nc):
    t_ch = SSM_CHUNK
    sw = tab_ref.shape[-1]

    @pl.when(pl.program_id(2) == 0)
    def _():
        car_ref[...] = jnp.zeros(car_ref.shape, F32)

    for t in range(t_ch):
        xcat_ref[:, t * LANES:(t + 1) * LANES] = u_ref[pl.ds(t, nc, stride=t_ch), :].astype(BF16)
    xcat = xcat_ref[...]
    y_in = _dot(xcat, m_ref[0])
    x = _dot(xcat, bc_ref[0])
    sre_ref[...] = x[:, 0:sw]
    sim_ref[...] = x[:, sw:2 * sw]
    first_row = lax.broadcasted_iota(jnp.int32, (SUBLANES, sw), 0) == 0

    def block(b, carry):
        cre, cim = carry
        r0 = pl.multiple_of(b * SUBLANES, SUBLANES)
        re = sre_ref[pl.ds(r0, SUBLANES), :]
        im = sim_ref[pl.ds(r0, SUBLANES), :]
        for n, k in enumerate((1, 2, 4)):
            tr, ti = tab_ref[0, 2 * n], tab_ref[0, 2 * n + 1]
            pre, pim = pltpu.roll(re, k, 0), pltpu.roll(im, k, 0)
            re, im = re + tr * pre - ti * pim, im + tr * pim + ti * pre
        pr, pi = tab_ref[0, 6], tab_ref[0, 7]
        ore = re + pr * cre - pi * cim
        oim = im + pr * cim + pi * cre
        sre_ref[pl.ds(r0, SUBLANES), :] = jnp.where(first_row, cre, pltpu.roll(ore, 1, 0))
        sim_ref[pl.ds(r0, SUBLANES), :] = jnp.where(first_row, cim, pltpu.roll(oim, 1, 0))
        last = SUBLANES - 1
        return (jnp.broadcast_to(ore[last:last + 1, :], (SUBLANES, sw)),
                jnp.broadcast_to(oim[last:last + 1, :], (SUBLANES, sw)))

    cre, cim = lax.fori_loop(0, nc // SUBLANES, block, (car_ref[0], car_ref[1]))
    car_ref[0] = cre
    car_ref[1] = cim
    sp = jnp.concatenate([sre_ref[...], sim_ref[...]], axis=1).astype(BF16)
    y = y_in + _dot(sp, cc_ref[0])
    for t in range(t_ch):
        y_ref[pl.ds(t, nc, stride=t_ch), :] = y[:, t * LANES:(t + 1) * LANES]


def _s5_scan(u, m, bc, cc, tab, *, batch):
    n, width = u.shape
    tiles = width // LANES
    seq = n // batch
    rs = min(S5_ROWS, seq)
    steps = seq // rs
    nc = rs // SSM_CHUNK
    sw = tab.shape[-1]
    kw = SSM_CHUNK * LANES
    blk = pl.BlockSpec((rs, LANES), lambda j, b, r: (b * steps + r, j))
    return pl.pallas_call(
        functools.partial(_s5_body, nc=nc),
        grid=(tiles, batch, steps),
        in_specs=[blk,
                  pl.BlockSpec((1, kw, kw), lambda j, b, r: (j, 0, 0)),
                  pl.BlockSpec((1, kw, 2 * sw), lambda j, b, r: (j, 0, 0)),
                  pl.BlockSpec((1, 2 * sw, kw), lambda j, b, r: (j, 0, 0)),
                  pl.BlockSpec((1, 8, SUBLANES, sw), lambda j, b, r: (j, 0, 0, 0))],
        out_specs=blk,
        out_shape=jax.ShapeDtypeStruct((n, width), F32),
        scratch_shapes=[pltpu.VMEM((nc, kw), BF16), pltpu.VMEM((nc, sw), F32), pltpu.VMEM((nc, sw), F32),
                        pltpu.VMEM((2, SUBLANES, sw), F32)],
        compiler_params=_params(("arbitrary", "arbitrary", "arbitrary")),
        name="s5_scan",
    )(u, m, bc, cc, tab)


def _ssm_tables(lam_re, lam_im, log_dt, b_re, b_im, c_re, c_im):
    hp = lax.Precision.HIGHEST
    g, p = lam_re.shape
    hh = b_re.shape[-1]
    t = SSM_CHUNK
    gq = LANES // hh
    tiles = g // gq
    dt = jnp.exp(log_dt)[:, None]
    mag = jnp.exp(lam_re * dt)
    ar, ai = mag * jnp.cos(lam_im * dt), mag * jnp.sin(lam_im * dt)
    den = lam_re * lam_re + lam_im * lam_im
    fr = ((ar - 1.0) * lam_re + ai * lam_im) / den
    fi = (ai * lam_re - (ar - 1.0) * lam_im) / den
    bbr = fr[..., None] * b_re - fi[..., None] * b_im
    bbi = fr[..., None] * b_im + fi[..., None] * b_re

    def powers(xr, xi, count):
        prs, pis = [jnp.ones_like(xr)], [jnp.zeros_like(xi)]
        for _ in range(count):
            prs.append(prs[-1] * xr - pis[-1] * xi)
            pis.append(prs[-2] * xi + pis[-1] * xr)
        return jnp.stack(prs), jnp.stack(pis)

    pr, pi = powers(ar, ai, t)
    car = c_re[None] * pr[:, :, None, :] - c_im[None] * pi[:, :, None, :]
    cai = c_im[None] * pr[:, :, None, :] + c_re[None] * pi[:, :, None, :]
    kern = (jnp.einsum('kgop,gpi->gkio', car[:t], bbr, precision=hp)
            - jnp.einsum('kgop,gpi->gkio', cai[:t], bbi, precision=hp))
    eye = jnp.eye(gq, dtype=F32)
    kw = t * gq * hh
    kq = kern.reshape(tiles, gq, t, hh, hh).transpose(0, 2, 1, 3, 4)
    lag_blk = (kq[:, :, :, :, None, :] * eye[None, None, :, None, :, None]).reshape(tiles, t, LANES, LANES)
    lag = jnp.arange(t)[None, :] - jnp.arange(t)[:, None]
    toep = jnp.where((lag >= 0)[None, :, :, None, None], lag_blk.astype(BF16)[:, jnp.clip(lag, 0)], 0)
    m = toep.transpose(0, 1, 3, 2, 4).reshape(tiles, kw, kw)
    rev = jnp.arange(t - 1, -1, -1)
    bcr = (pr[rev][..., None] * bbr[None] - pi[rev][..., None] * bbi[None]).transpose(1, 0, 3, 2)
    bci = (pr[rev][..., None] * bbi[None] + pi[rev][..., None] * bbr[None]).transpose(1, 0, 3, 2)
    bri = jnp.stack([bcr, bci], axis=3).reshape(tiles, gq, t, hh, 2, p)
    bc = (bri.transpose(0, 2, 1, 3, 4, 5)[:, :, :, :, :, None, :]
          * eye[None, None, :, None, None, :, None]).reshape(tiles, kw, 2 * gq * p)
    ccr = car[1:].transpose(1, 3, 0, 2)
    cci = -cai[1:].transpose(1, 3, 0, 2)
    cri = jnp.stack([ccr, cci], axis=1).reshape(tiles, gq, 2, p, t, hh)
    cc = (cri.transpose(0, 2, 1, 3, 4, 5)[:, :, :, :, :, None, :]
          * eye[None, None, :, None, None, :, None]).reshape(tiles, 2 * gq * p, kw)
    a16r, a16i = pr[t].reshape(tiles, gq * p), pi[t].reshape(tiles, gq * p)
    qr, qi = powers(a16r, a16i, SUBLANES)
    sub = jnp.arange(SUBLANES)
    tabs = []
    for k in (1, 2, 4):
        keep = (sub >= k).astype(F32)[None, :, None]
        tabs += [qr[k][:, None, :] * keep, qi[k][:, None, :] * keep]
    tabs += [qr[1:].transpose(1, 0, 2), qi[1:].transpose(1, 0, 2)]
    tab = jnp.stack(tabs, axis=1)
    return m.astype(BF16), bc.astype(BF16), cc.astype(BF16), tab


def _attn_body(q_ref, k_ref, v_ref, o_ref, *, tb, nsub):
    i = pl.program_id(1)
    qs = [q_ref[0, s * tb:(s + 1) * tb, :] for s in range(nsub)]
    diag = lax.broadcasted_iota(jnp.int32, (tb, tb), 1) <= lax.broadcasted_iota(jnp.int32, (tb, tb), 0)

    def load(j):
        k0 = pl.multiple_of(j * tb, tb)
        return k_ref[0, pl.ds(k0, tb), :], v_ref[pl.ds(k0, tb), :]

    def step(q, kb, vb, carry, masked):
        m, l, acc = carry
        s = lax.dot_general(q, kb, (((1,), (1,)), ((), ())), preferred_element_type=F32)
        if masked:
            s = jnp.where(diag, s, NEG_BIG)
        m_new = jnp.maximum(m, jnp.max(s, axis=-1, keepdims=True))
        p = jnp.exp2(s - m_new)
        alpha = jnp.exp2(m - m_new)
        l = alpha * l + jnp.sum(p, axis=-1, keepdims=True)
        acc = alpha * acc + _dot(p.astype(BF16), vb)
        return m_new, l, acc

    def body(j, carries):
        kb, vb = load(j)
        return tuple(step(qs[s], kb, vb, carries[s], False) for s in range(nsub))

    init = (jnp.full((tb, 1), NEG_BIG, F32), jnp.zeros((tb, 1), F32), jnp.zeros((tb, V_HEAD), F32))
    n_full = i * nsub
    carries = list(lax.fori_loop(0, n_full, body, (init,) * nsub))
    for d in range(nsub):
        kb, vb = load(n_full + d)
        for s in range(d, nsub):
            carries[s] = step(qs[s], kb, vb, carries[s], s == d)
    for s in range(nsub):
        _, l, acc = carries[s]
        o_ref[s * tb:(s + 1) * tb, :] = (acc / l).astype(BF16)


def _mla_attn(q, k, v, *, batch, seq, heads):
    tb, nsub = ATTN_TB, ATTN_SUB
    tq = tb * nsub
    nq = seq // tq
    n = batch * seq
    return pl.pallas_call(
        functools.partial(_attn_body, tb=tb, nsub=nsub),
        grid=(batch * heads, nq),
        in_specs=[pl.BlockSpec((1, tq, 2 * LANES), lambda bh, i: (bh % heads, (bh // heads) * nq + i, 0)),
                  pl.BlockSpec((1, seq, 2 * LANES), lambda bh, i: (bh % heads, bh // heads, 0)),
                  pl.BlockSpec((seq, V_HEAD), lambda bh, i: (bh // heads, bh % heads))],
        out_specs=pl.BlockSpec((tq, V_HEAD), lambda bh, i: ((bh // heads) * nq + i, bh % heads)),
        out_shape=jax.ShapeDtypeStruct((n, heads * V_HEAD), BF16),
        compiler_params=_params(("arbitrary", "arbitrary")),
        name="mla_attn",
    )(q, k, v)


def _gelu_tanh(x):
    return 0.5 * x * (1.0 + jnp.tanh(math.sqrt(2.0 / math.pi) * (x + 0.044715 * (x * x * x))))


def _mix_out_body(x_ref, yp_ref, ysc_ref, us_ref, ym_ref, mod_ref, dsk_ref, wglu_ref, gns_ref, gnm_ref,
                  wo_ref, g2_ref, wrh_ref, wrl_ref, br_ref,
                  x1_ref, h2_ref, ridx_ref, rw_ref, *, n_exp, n_grp):
    mod = mod_ref[0]
    g_a, sh_f, sc_f = mod[2:3], mod[3:4], mod[4:5]
    ys = _gelu_tanh(ysc_ref[...] + dsk_ref[...] * us_ref[...])
    ys = ys * jax.nn.sigmoid(_dot(ys.astype(BF16), wglu_ref[...]))
    ysn = _rms(ys, gns_ref[...]).astype(BF16)
    ymn = _rms(ym_ref[...].astype(F32), gnm_ref[...]).astype(BF16)
    wp, ws = yp_ref.shape[1], ysc_ref.shape[1]
    o = (_dot(yp_ref[...], wo_ref[0:wp, :]) + _dot(ysn, wo_ref[wp:wp + ws, :])
         + _dot(ymn, wo_ref[wp + ws:, :]))
    x1 = x_ref[...] + g_a * o
    x1_ref[...] = x1
    h2 = _rms(x1, g2_ref[...]) * (1.0 + sc_f) + sh_f
    h2_ref[...] = h2

    hh, hl = _split_bf16(h2)
    logits = _dot(hh, wrh_ref[...]) + _dot(hl, wrh_ref[...]) + _dot(hh, wrl_ref[...]) + br_ref[...]
    lane = lax.broadcasted_iota(jnp.int32, logits.shape, 1).astype(F32)
    far = float(2 * LANES)
    per = n_exp // n_grp
    gl = jnp.where(lane >= n_exp, jnp.where(lane < n_exp + n_grp, logits, NEG_BIG), NEG_BIG)
    gmax = jnp.max(gl, axis=-1, keepdims=True)
    gtop = jnp.min(jnp.where(gl == gmax, lane, far), axis=-1, keepdims=True) - n_exp
    g_w = 1.0 / jnp.sum(jnp.exp(gl - gmax), axis=-1, keepdims=True)
    lo = gtop * per
    el = jnp.where(lane >= lo, jnp.where(lane < lo + per, logits, NEG_BIG), NEG_BIG)
    m1 = jnp.max(el, axis=-1, keepdims=True)
    i1 = jnp.min(jnp.where(el == m1, lane, far), axis=-1, keepdims=True)
    el2 = jnp.where(lane == i1, NEG_BIG, el)
    m2 = jnp.max(el2, axis=-1, keepdims=True)
    i2 = jnp.min(jnp.where(el2 == m2, lane, far), axis=-1, keepdims=True)
    e21 = jnp.exp(m2 - m1)
    w1 = g_w / (1.0 + e21)
    w2 = g_w * e21 / (1.0 + e21)
    ridx_ref[...] = jnp.where(lane == 0.0, i1, jnp.where(lane == 1.0, i2, 0.0)).astype(jnp.int32)
    rw_ref[...] = jnp.where(lane == 0.0, w1, jnp.where(lane == 1.0, w2, 0.0))


def _mix_out(x, ypool, yscan, ussm, ymla, mod, dsk, wglu, gns, gnm, wo, g2, wrh, wrl, br, *, seq, n_exp, n_grp):
    n, d = x.shape
    tm = TOKEN_TILE
    tpb = seq // tm
    row = lambda w: pl.BlockSpec((tm, w), lambda i: (i, 0))
    return pl.pallas_call(
        functools.partial(_mix_out_body, n_exp=n_exp, n_grp=n_grp),
        grid=(n // tm,),
        in_specs=[row(d), row(ypool.shape[1]), row(yscan.shape[1]), row(ussm.shape[1]), row(ymla.shape[1]),
                  pl.BlockSpec((1, 6, d), lambda i: (i // tpb, 0, 0)),
                  _const_spec(dsk.shape), _const_spec(wglu.shape), _const_spec(gns.shape), _const_spec(gnm.shape),
                  _const_spec(wo.shape), _const_spec(g2.shape), _const_spec(wrh.shape), _const_spec(wrl.shape),
                  _const_spec(br.shape)],
        out_specs=[row(d), row(d), row(LANES), row(LANES)],
        out_shape=[jax.ShapeDtypeStruct((n, d), F32), jax.ShapeDtypeStruct((n, d), F32),
                   jax.ShapeDtypeStruct((n, LANES), jnp.int32), jax.ShapeDtypeStruct((n, LANES), F32)],
        compiler_params=_params(("arbitrary",)),
        name="mix_out",
    )(x, ypool, yscan, ussm, ymla, mod, dsk, wglu, gns, gnm, wo, g2, wrh, wrl, br)


def _route_plan(ridx, n_exp, blk):
    expert = ridx[:, :TOP_K].reshape(-1)
    a = expert.shape[0]
    onehot = (expert[:, None] == jnp.arange(n_exp, dtype=jnp.int32)[None, :]).astype(jnp.int32)
    seg = 256
    oh3 = onehot.astype(F32).reshape(a // seg, seg, n_exp)
    intra = jnp.einsum('ij,bje->bie', jnp.tril(jnp.ones((seg, seg), F32)), oh3)
    tot = intra[:, -1, :]
    csum = (intra + (jnp.cumsum(tot, axis=0) - tot)[:, None, :]).reshape(a, n_exp).astype(jnp.int32)
    counts = csum[-1]
    padded = (counts + blk - 1) // blk * blk
    pad_end = jnp.cumsum(padded)
    pad_start = pad_end - padded
    dest = jnp.sum(onehot * (csum - 1 + pad_start[None, :]), axis=1).astype(jnp.int32)
    nb = a // blk + n_exp
    bstart = jnp.arange(nb, dtype=jnp.int32) * blk
    be = jnp.minimum(jnp.sum((pad_end[None, :] <= bstart[:, None]).astype(jnp.int32), axis=1), n_exp - 1)
    nv = jnp.clip(counts[be] - (bstart - pad_start[be]), 0, blk).astype(jnp.int32)
    return dest, be, nv, nb


def _dispatch_body(nv_ref, dest_ref, h_ref, xs_hbm, zero_ref, sem, zsem, *, tm, blk, nb):
    def zero_copy(j):
        return pltpu.make_async_copy(zero_ref, xs_hbm.at[pl.ds(j * blk, blk)], zsem)

    @pl.when(pl.program_id(0) == 0)
    def _():
        zero_ref[...] = jnp.zeros(zero_ref.shape, zero_ref.dtype)

        def fill(j, carry):
            @pl.when(nv_ref[j] < blk)
            def _():
                zero_copy(j).start()
            return carry

        def drain(j, carry):
            @pl.when(nv_ref[j] < blk)
            def _():
                zero_copy(j).wait()
            return carry

        lax.fori_loop(0, nb, fill, 0)
        lax.fori_loop(0, nb, drain, 0)

    def issue(r, carry):
        for k in range(TOP_K):
            pltpu.make_async_copy(h_ref.at[pl.ds(r, 1)], xs_hbm.at[pl.ds(dest_ref[TOP_K * r + k], 1)], sem).start()
        return carry

    lax.fori_loop(0, tm, issue, 0, unroll=ROW_DMA_UNROLL)
    for k in range(TOP_K):
        pltpu.make_async_copy(h_ref, xs_hbm.at[pl.ds(0, tm)], sem).wait()


def _dispatch(nv, dest, h2, nb):
    n, d = h2.shape
    tm = TOKEN_TILE
    blk = MOE_ROWS
    grid_spec = pltpu.PrefetchScalarGridSpec(
        num_scalar_prefetch=1,
        grid=(n // tm,),
        in_specs=[pl.BlockSpec((TOP_K * tm,), lambda i, nv: (i,), memory_space=pltpu.SMEM),
                  pl.BlockSpec((tm, d), lambda i, nv: (i, 0))],
        out_specs=pl.BlockSpec(memory_space=pl.ANY),
        scratch_shapes=[pltpu.VMEM((blk, d), h2.dtype), pltpu.SemaphoreType.DMA(()), pltpu.SemaphoreType.DMA(())],
    )
    return pl.pallas_call(
        functools.partial(_dispatch_body, tm=tm, blk=blk, nb=nb),
        grid_spec=grid_spec,
        out_shape=jax.ShapeDtypeStruct((nb * blk, d), h2.dtype),
        compiler_params=pltpu.CompilerParams(dimension_semantics=("arbitrary",), vmem_limit_bytes=VMEM_LIMIT,
                                             has_side_effects=True),
        name="moe_dispatch",
    )(nv, dest, h2)


def _expert_body(be_ref, nv_ref, xs_ref, w1_ref, w3_ref, w2_ref, ys_ref, w1b, w3b, w2b, *, blk):
    j = pl.program_id(0)
    e = be_ref[j]
    prev = be_ref[jnp.maximum(j - 1, 0)]

    @pl.when(jnp.logical_or(j == 0, e != prev))
    def _():
        w1b[...] = w1_ref[0, 0].astype(BF16)
        w3b[...] = w3_ref[0, 0].astype(BF16)
        w2b[...] = w2_ref[0, 0].astype(BF16)

    nv = nv_ref[j]

    @pl.when(nv > 0)
    def _():
        x = xs_ref[...].astype(BF16)
        a = _dot(x, w1b[...])
        b = _dot(x, w3b[...])
        ys_ref[...] = _dot((a * jax.nn.sigmoid(a) * b).astype(BF16), w2b[...])

    @pl.when(nv == 0)
    def _():
        ys_ref[...] = jnp.zeros(ys_ref.shape, F32)


def _experts(be, nv, xs, w1, w3, w2, nb, layer):
    blk = MOE_ROWS
    _, _, d, de = w1.shape
    grid_spec = pltpu.PrefetchScalarGridSpec(
        num_scalar_prefetch=2,
        grid=(nb,),
        in_specs=[pl.BlockSpec((blk, d), lambda j, be, nv: (j, 0)),
                  pl.BlockSpec((1, 1, d, de), lambda j, be, nv: (layer, be[j], 0, 0)),
                  pl.BlockSpec((1, 1, d, de), lambda j, be, nv: (layer, be[j], 0, 0)),
                  pl.BlockSpec((1, 1, de, d), lambda j, be, nv: (layer, be[j], 0, 0))],
        out_specs=pl.BlockSpec((blk, d), lambda j, be, nv: (j, 0)),
        scratch_shapes=[pltpu.VMEM((d, de), BF16), pltpu.VMEM((d, de), BF16), pltpu.VMEM((de, d), BF16)],
    )
    return pl.pallas_call(
        functools.partial(_expert_body, blk=blk),
        grid_spec=grid_spec,
        out_shape=jax.ShapeDtypeStruct((nb * blk, d), F32),
        compiler_params=_params(("arbitrary",)),
        name="moe_experts",
    )(be, nv, xs, w1, w3, w2)


def _combine_body(dest_ref, x1_ref, rw_ref, mod_ref, fg_ref, ys_hbm, out_ref, buf, sem, *, tm, final):
    def issue(r, carry):
        for k in range(TOP_K):
            pltpu.make_async_copy(ys_hbm.at[pl.ds(dest_ref[TOP_K * r + k], 1)], buf.at[k, pl.ds(r, 1)], sem).start()
        return carry

    lax.fori_loop(0, tm, issue, 0, unroll=ROW_DMA_UNROLL)
    for k in range(TOP_K):
        pltpu.make_async_copy(ys_hbm.at[pl.ds(0, tm)], buf.at[k], sem).wait()
    w = rw_ref[...]
    y = buf[0] * w[:, 0:1] + buf[1] * w[:, 1:2]
    x2 = x1_ref[...] + mod_ref[0][5:6] * y
    out_ref[...] = _rms(x2, fg_ref[...]) if final else x2


def _combine(dest, x1, rw, mod, fg, ys, *, seq, final):
    n, d = x1.shape
    tm = TOKEN_TILE
    tpb = seq // tm
    row = lambda w: pl.BlockSpec((tm, w), lambda i: (i, 0))
    return pl.pallas_call(
        functools.partial(_combine_body, tm=tm, final=final),
        grid=(n // tm,),
        in_specs=[pl.BlockSpec((TOP_K * tm,), lambda i: (i,), memory_space=pltpu.SMEM),
                  row(d), row(LANES),
                  pl.BlockSpec((1, 6, d), lambda i: (i // tpb, 0, 0)),
                  pl.BlockSpec((1, d), lambda i: (0, 0)),
                  pl.BlockSpec(memory_space=pl.ANY)],
        out_specs=row(d),
        out_shape=jax.ShapeDtypeStruct((n, d), F32),
        scratch_shapes=[pltpu.VMEM((TOP_K, tm, d), F32), pltpu.SemaphoreType.DMA(())],
        compiler_params=_params(("arbitrary",)),
        name="moe_combine",
    )(dest, x1, rw, mod, fg, ys)


def _rot_half_cols(w):
    half = w.shape[-1] // 2
    return jnp.concatenate([-w[..., half:], w[..., :half]], axis=-1)


def _layer_weights(l, w_in, pool_w, w_uq, w_ukv, w_out, ssm_w_glu, router_w_group, router_b_group,
                   router_w_expert, router_b_expert):
    d = w_in.shape[1]
    heads = w_uq.shape[2]
    wi = w_in[l]
    kpe = wi[:, -QK_ROPE:]
    pad = jnp.zeros((d, LANES - QK_ROPE), F32)
    win = jnp.concatenate([wi[:, :-QK_ROPE], kpe, pad, _rot_half_cols(kpe), pad], axis=1).astype(BF16)
    uq = w_uq[l]
    r = uq.shape[0]
    zq = jnp.zeros((r, heads, LANES - QK_ROPE), F32)
    wuq = jnp.concatenate([uq, zq], axis=-1).reshape(r, heads * 2 * LANES).astype(BF16)
    wuqr = jnp.concatenate([_rot_half_cols(uq[..., QK_NOPE:]), zq], axis=-1).reshape(r, heads * LANES).astype(BF16)
    ukv = w_ukv[l]
    wukv = jnp.concatenate([ukv[..., :QK_NOPE].reshape(ukv.shape[0], -1),
                            ukv[..., QK_NOPE:].reshape(ukv.shape[0], -1)], axis=1).astype(BF16)
    n_grp = router_w_group.shape[-1]
    n_exp = router_w_expert.shape[-1]
    wr = jnp.concatenate([router_w_expert[l], router_w_group[l],
                          jnp.zeros((d, LANES - n_exp - n_grp), F32)], axis=1)
    wrh, wrl = _split_bf16(wr)
    br = jnp.concatenate([router_b_expert[l], router_b_group[l],
                          jnp.zeros((LANES - n_exp - n_grp,), F32)]).reshape(1, LANES)
    return dict(win=win, poolw=pool_w[l].astype(BF16), wuq=wuq, wuqr=wuqr, wukv=wukv,
                wo=w_out[l].astype(BF16), wglu=ssm_w_glu[l].astype(BF16), wrh=wrh, wrl=wrl, br=br)


def kernel(x, c, positions, w_ada, b_ada, norm1_g, w_in, pool_w, pool_scale, ssm_lam_re, ssm_lam_im, ssm_log_dt, ssm_b_re, ssm_b_im, ssm_c_re, ssm_c_im, ssm_d, ssm_w_glu, q_norm_g, kv_norm_g, w_uq, w_ukv, out_norm_g, w_out, norm2_g, router_w_group, router_b_group, router_w_expert, router_b_expert, w_gate, w_up, w_down, final_g):
    batch, seq, d = x.shape
    depth = w_ada.shape[0]
    n = batch * seq
    heads = w_uq.shape[2]
    n_grp = router_w_group.shape[-1]
    n_exp = router_w_expert.shape[-1]
    pw = pool_w.shape[1] * pool_w.shape[2]
    sw = ssm_d.shape[-1]
    assert SSM_CHUNK * SSM_GROUP == MXU_DIM and pw == len(POOL_WINDOWS) * LANES and sw % LANES == 0
    assert seq % (ATTN_TB * ATTN_SUB) == 0 and seq % TOKEN_TILE == 0 and (n * TOP_K) % MOE_ROWS == 0

    mod_all = _ada_mod(c, w_ada, b_ada)
    cs, sn = _rope_tables(positions)
    xf = x.reshape(n, d)
    for l in range(depth):
        wts = _layer_weights(l, w_in, pool_w, w_uq, w_ukv, w_out, ssm_w_glu, router_w_group, router_b_group,
                             router_w_expert, router_b_expert)
        mod = mod_all[l]
        gn = out_norm_g[l]
        ypool, ussm, q, k, v = _mix_in(
            xf, mod, norm1_g[l].reshape(1, d), wts['win'], wts['poolw'], pool_scale[l].reshape(1, pw),
            gn[:pw].reshape(1, pw), q_norm_g[l].reshape(1, -1), kv_norm_g[l].reshape(1, -1),
            wts['wuq'], wts['wuqr'], wts['wukv'], cs, sn, seq=seq, heads=heads)
        m, bc, cc, tab = _ssm_tables(ssm_lam_re[l], ssm_lam_im[l], ssm_log_dt[l], ssm_b_re[l], ssm_b_im[l],
                                     ssm_c_re[l], ssm_c_im[l])
        yscan = _s5_scan(ussm, m, bc, cc, tab, batch=batch)
        ymla = _mla_attn(q, k, v, batch=batch, seq=seq, heads=heads)
        x1, h2, ridx, rw = _mix_out(
            xf, ypool, yscan, ussm, ymla, mod, ssm_d[l].reshape(1, sw), wts['wglu'],
            gn[pw:pw + sw].reshape(1, sw), gn[pw + sw:].reshape(1, -1), wts['wo'], norm2_g[l].reshape(1, d),
            wts['wrh'], wts['wrl'], wts['br'], seq=seq, n_exp=n_exp, n_grp=n_grp)
        dest, be, nv, nb = _route_plan(ridx, n_exp, MOE_ROWS)
        xs = _dispatch(nv, dest, h2, nb)
        ys = _experts(be, nv, xs, w_gate, w_up, w_down, nb, l)
        xf = _combine(dest, x1, rw, mod, final_g.reshape(1, d), ys, seq=seq, final=(l == depth - 1))
    return xf.reshape(batch, seq, d)
```
